```python
import jax
import jax.numpy as jnp
from jax import lax
import numpy as np

D_MODEL = 1024
BATCH = 8
SEQ = 4096
DEPTH = 2

GRID_W = 64
CTX_LEN = 256
CHUNK = 64
GLA_HEADS = 4
GLA_DK = 64
GLA_DV = 128
GLA_GATE_RANK = 16
GLA_GATE_TAU = 16.0
ATT_Q_HEADS = 8
ATT_KV_HEADS = 2
ATT_HEAD_DIM = 64
ATT_GROUP = ATT_Q_HEADS // ATT_KV_HEADS
ROPE_AXIS_DIM = ATT_HEAD_DIM // 2
ROPE_THETA = 10000.0
Q_BLOCK = 128
HGRN_HEADS = 8
HGRN_DF = 128
HGRN_DV = D_MODEL // HGRN_HEADS
D_FF = ((8 * D_MODEL + 3 * 256 - 1) // (3 * 256)) * 256

EVEN_SIZES = (GLA_HEADS * GLA_DK, GLA_HEADS * GLA_DK, GLA_HEADS * GLA_DV, GLA_HEADS * GLA_DV, 2 * GLA_GATE_RANK,
              ATT_Q_HEADS * ATT_HEAD_DIM, ATT_KV_HEADS * ATT_HEAD_DIM, ATT_KV_HEADS * ATT_HEAD_DIM)
ODD_SIZES = (HGRN_HEADS * HGRN_DF,) * 3 + (HGRN_HEADS * HGRN_DV,) * 2
EVEN_MIX = GLA_HEADS * GLA_DV + ATT_Q_HEADS * ATT_HEAD_DIM
ODD_MIX = HGRN_HEADS * HGRN_DV

kernel_name = 'hybrid_gla_gqa_hgrn2_prefix_dit'


def _rms(x, gain, eps=1e-6):
    xf = x.astype(jnp.float32)
    y = xf * lax.rsqrt(jnp.mean(xf * xf, axis=-1, keepdims=True) + eps)
    return (y * gain.astype(jnp.float32)).astype(x.dtype)


def _modulate(h, shift, scale):
    return h * (1 + scale) + shift


def _split(a, sizes):
    return jnp.split(a, np.cumsum(sizes)[:-1].tolist(), axis=-1)


def _heads(a, n_heads):
    B, N, _ = a.shape
    return a.reshape(B, N, n_heads, -1).transpose(0, 2, 1, 3)


def _merge(a):
    B, H, N, d = a.shape
    return a.transpose(0, 2, 1, 3).reshape(B, N, H * d)


def _gated_readout(o, og, gain):
    return _merge(_rms(o, gain) * jax.nn.silu(og))


def _swiglu(h, w_gate, w_up, w_down):
    return (jax.nn.silu(h @ w_gate) * (h @ w_up)) @ w_down


def _chunk_gla(q, k, v, g, s0):
    B, H, T, K = q.shape
    V = v.shape[-1]
    n = T // CHUNK

    def blocks(a):
        return jnp.moveaxis(a.astype(jnp.float32).reshape(B, H, n, CHUNK, a.shape[-1]), 2, 0)

    tri = jnp.tril(jnp.ones((CHUNK, CHUNK), dtype=bool))[:, :, None]

    def step(s, blk):
        qc, kc, vc, gc = blk
        G = jnp.cumsum(gc, axis=2)
        decay = jnp.exp(jnp.where(tri, G[:, :, :, None, :] - G[:, :, None, :, :], -jnp.inf))
        scores = jnp.einsum('bhik,bhjk,bhijk->bhij', qc, kc, decay)
        o = jnp.einsum('bhij,bhjv->bhiv', scores, vc) + jnp.einsum('bhik,bhkv->bhiv', qc * jnp.exp(G), s)
        G_last = G[:, :, -1:, :]
        s = s * jnp.exp(G_last)[:, :, 0, :, None] + jnp.einsum('bhjk,bhjv->bhkv', kc * jnp.exp(G_last - G), vc)
        return s, o

    s, o = lax.scan(step, s0, (blocks(q), blocks(k), blocks(v), blocks(g)))
    return jnp.moveaxis(o, 0, 2).reshape(B, H, T, V).astype(v.dtype), s


def _directional_scan(q, k, v, g, n_ctx, reverse):
    def seg(a, sl):
        a = a[:, :, sl]
        return jnp.flip(a, 2) if reverse else a
    c_sl, t_sl = slice(0, n_ctx), slice(n_ctx, None)
    B, H, _, K = q.shape
    s0 = jnp.zeros((B, H, K, v.shape[-1]), jnp.float32)
    o_ctx, s_ctx = _chunk_gla(seg(q, c_sl), seg(k, c_sl), seg(v, c_sl), seg(g, c_sl), s0)
    o_lat, _ = _chunk_gla(seg(q, t_sl), seg(k, t_sl), seg(v, t_sl), seg(g, t_sl), s_ctx)
    if reverse:
        o_ctx, o_lat = jnp.flip(o_ctx, 2), jnp.flip(o_lat, 2)
    return o_ctx, o_lat


def _axial_rope_tables(n_tokens):
    rows_n = n_tokens // GRID_W
    row = jnp.repeat(jnp.arange(rows_n), GRID_W).astype(jnp.float32)
    col = jnp.tile(jnp.arange(GRID_W), rows_n).astype(jnp.float32)
    inv = ROPE_THETA ** (-jnp.arange(0, ROPE_AXIS_DIM, 2, dtype=jnp.float32) / ROPE_AXIS_DIM)
    ang = jnp.concatenate([row[:, None] * inv, col[:, None] * inv], axis=-1)
    return jnp.cos(ang), jnp.sin(ang)


def _rope(x, cos, sin):
    xf = x.astype(jnp.float32).reshape(x.shape[:-1] + (-1, 2))
    x0, x1 = xf[..., 0], xf[..., 1]
    out = jnp.stack([x0 * cos - x1 * sin, x0 * sin + x1 * cos], axis=-1)
    return out.reshape(x.shape).astype(x.dtype)


def _softmax_attend(q, k, v):
    s = jnp.einsum('bhgqd,bhkd->bhgqk', q, k, preferred_element_type=jnp.float32) * (ATT_HEAD_DIM ** -0.5)
    p = jax.nn.softmax(s, axis=-1).astype(v.dtype)
    return jnp.einsum('bhgqk,bhkd->bhgqd', p, v)


def _even_mixer(h_ctx, h_lat, w_in, gla_w_gate, gla_b_gate, gla_out_norm, att_q_norm, att_k_norm, cos, sin, ctx_out):
    B, n_ctx, _ = h_ctx.shape
    h = jnp.concatenate([h_ctx, h_lat], axis=1)
    N = h.shape[1]
    T = N - n_ctx
    gq, gk, gv, gog, gz, aq, ak, av = _split(h @ w_in, EVEN_SIZES)

    q = _heads(gq, GLA_HEADS) * (GLA_DK ** -0.5)
    k = _heads(gk, GLA_HEADS)
    v = _heads(gv, GLA_HEADS)
    og = _heads(gog, GLA_HEADS)
    dirs = []
    for d, z in enumerate(jnp.split(gz, 2, axis=-1)):
        logit = (z @ gla_w_gate[d] + gla_b_gate[d]).astype(jnp.float32)
        g = _heads(jax.nn.log_sigmoid(logit) / GLA_GATE_TAU, GLA_HEADS)
        dirs.append(_directional_scan(q, k, v, g, n_ctx, reverse=(d == 1)))
    a_lat = _gated_readout(dirs[0][1] + dirs[1][1], og[:, :, n_ctx:], gla_out_norm)

    qa = _rms(aq.reshape(B, N, ATT_KV_HEADS, ATT_GROUP, ATT_HEAD_DIM), att_q_norm).transpose(0, 2, 3, 1, 4)
    ka = _rms(ak.reshape(B, N, ATT_KV_HEADS, ATT_HEAD_DIM), att_k_norm).transpose(0, 2, 1, 3)
    va = av.reshape(B, N, ATT_KV_HEADS, ATT_HEAD_DIM).transpose(0, 2, 1, 3)
    q_lat = _rope(qa[:, :, :, n_ctx:], cos, sin)
    k_all = jnp.concatenate([ka[:, :, :n_ctx], _rope(ka[:, :, n_ctx:], cos, sin)], axis=2)
    q_blocks = jnp.moveaxis(q_lat.reshape(B, ATT_KV_HEADS, ATT_GROUP, T // Q_BLOCK, Q_BLOCK, ATT_HEAD_DIM), 3, 0)
    o_blocks = lax.map(lambda qb: _softmax_attend(qb, k_all, va), q_blocks)
    b_lat = jnp.moveaxis(o_blocks, 0, 3).reshape(B, ATT_KV_HEADS, ATT_GROUP, T, ATT_HEAD_DIM)
    b_lat = b_lat.transpose(0, 3, 1, 2, 4).reshape(B, T, ATT_Q_HEADS * ATT_HEAD_DIM)
    o_lat = jnp.concatenate([a_lat, b_lat], axis=-1)
    if not ctx_out:
        return None, o_lat
    a_ctx = _gated_readout(dirs[0][0] + dirs[1][0], og[:, :, :n_ctx], gla_out_norm)
    b_ctx = _softmax_attend(qa[:, :, :, :n_ctx], ka[:, :, :n_ctx], va[:, :, :n_ctx])
    b_ctx = b_ctx.transpose(0, 3, 1, 2, 4).reshape(B, n_ctx, ATT_Q_HEADS * ATT_HEAD_DIM)
    return jnp.concatenate([a_ctx, b_ctx], axis=-1), o_lat


def _odd_mixer(h_ctx, h_lat, w_in, lower_bounds, layer, out_norm, ctx_out):
    n_ctx = h_ctx.shape[1]
    h = jnp.concatenate([h_ctx, h_lat], axis=1)
    fq, f_fwd, f_bwd, fi, fog = _split(h @ w_in, ODD_SIZES)
    q = _heads(jax.nn.silu(fq), HGRN_HEADS)
    i = _heads(fi, HGRN_HEADS)
    og = _heads(fog, HGRN_HEADS)
    lbs = jnp.cumsum(jax.nn.softmax(lower_bounds.astype(jnp.float32), axis=1), axis=1)
    lb = (lbs[:, layer] - lbs[:, 0]).reshape(2, HGRN_HEADS, 1, HGRN_DF)
    dirs = []
    for d, f in enumerate((f_fwd, f_bwd)):
        log_f = jnp.logaddexp(jnp.log(lb[d]), jnp.log1p(-lb[d]) + jax.nn.log_sigmoid(_heads(f, HGRN_HEADS).astype(jnp.float32)))
        k = -jnp.expm1(log_f)
        dirs.append(_directional_scan(q, k, i, log_f, n_ctx, reverse=(d == 1)))
    o_lat = _gated_readout(dirs[0][1] + dirs[1][1], og[:, :, n_ctx:], out_norm)
    if not ctx_out:
        return None, o_lat
    return _gated_readout(dirs[0][0] + dirs[1][0], og[:, :, :n_ctx], out_norm), o_lat


def setup_inputs(seed: int = 0) -> dict:
    key = jax.random.key(seed)
    ks = iter(jax.random.split(key, 32))
    n_even, n_odd = (DEPTH + 1) // 2, DEPTH // 2
    D = D_MODEL

    def nrm(shape, scale):
        return scale * jax.random.normal(next(ks), shape, jnp.float32)

    def gain(shape):
        return 1.0 + nrm(shape, 0.02)

    return {
        'x': nrm((BATCH, SEQ, D), 1.0),
        'c': nrm((BATCH, D), 1.0),
        'ctx': nrm((BATCH, CTX_LEN, D), 1.0),
        'c_ctx': nrm((D,), 1.0),
        'mod_w': nrm((DEPTH, D, 6 * D), D ** -0.5),
        'mod_b': nrm((DEPTH, 6 * D), 0.02),
        'norm_pre_mix': gain((DEPTH, D)),
        'norm_post_mix': gain((DEPTH, D)),
        'norm_pre_ffn': gain((DEPTH, D)),
        'norm_post_ffn': gain((DEPTH, D)),
        'even_w_in': nrm((n_even, D, sum(EVEN_SIZES)), D ** -0.5),
        'gla_w_gate': nrm((n_even, 2, GLA_GATE_RANK, GLA_HEADS * GLA_DK), GLA_GATE_RANK ** -0.5),
        'gla_b_gate': nrm((n_even, 2, GLA_HEADS * GLA_DK), 0.02),
        'gla_out_norm': gain((n_even, GLA_DV)),
        'att_q_norm': gain((n_even, ATT_HEAD_DIM)),
        'att_k_norm': gain((n_even, ATT_HEAD_DIM)),
        'even_w_out': nrm((n_even, EVEN_MIX, D), EVEN_MIX ** -0.5),
        'odd_w_in': nrm((n_odd, D, sum(ODD_SIZES)), D ** -0.5),
        'hgrn_lower_bounds': nrm((2, DEPTH, HGRN_HEADS * HGRN_DF), 0.1),
        'hgrn_out_norm': gain((n_odd, HGRN_DV)),
        'odd_w_out': nrm((n_odd, ODD_MIX, D), ODD_MIX ** -0.5),
        'ffn_w_gate': nrm((DEPTH, D, D_FF), D ** -0.5),
        'ffn_w_up': nrm((DEPTH, D, D_FF), D ** -0.5),
        'ffn_w_down': nrm((DEPTH, D_FF, D), D_FF ** -0.5),
    }


def reference(x, c, ctx, c_ctx, mod_w, mod_b, norm_pre_mix, norm_post_mix, norm_pre_ffn, norm_post_ffn,
              even_w_in, gla_w_gate, gla_b_gate, gla_out_norm, att_q_norm, att_k_norm, even_w_out,
              odd_w_in, hgrn_lower_bounds, hgrn_out_norm, odd_w_out, ffn_w_gate, ffn_w_up, ffn_w_down):
    cos, sin = _axial_rope_tables(x.shape[1])
    sc = jax.nn.silu(c)
    scc = jax.nn.silu(c_ctx)
    x_ctx, x_lat = ctx, x
    for l in range(DEPTH):
        last = l == DEPTH - 1
        j = l // 2
        m_lat = jnp.split((sc @ mod_w[l] + mod_b[l])[:, None, :], 6, axis=-1)
        m_ctx = jnp.split((scc @ mod_w[l] + mod_b[l])[None, None, :], 6, axis=-1)
        h_ctx = _modulate(_rms(x_ctx, norm_pre_mix[l]), m_ctx[0], m_ctx[1])
        h_lat = _modulate(_rms(x_lat, norm_pre_mix[l]), m_lat[0], m_lat[1])
        if l % 2 == 0:
            o_ctx, o_lat = _even_mixer(h_ctx, h_lat, even_w_in[j], gla_w_gate[j], gla_b_gate[j], gla_out_norm[j],
                                       att_q_norm[j], att_k_norm[j], cos, sin, not last)
            w_out = even_w_out[j]
        else:
            o_ctx, o_lat = _odd_mixer(h_ctx, h_lat, odd_w_in[j], hgrn_lower_bounds, l, hgrn_out_norm[j], not last)
            w_out = odd_w_out[j]
        x_lat = x_lat + m_lat[2] * _rms(o_lat @ w_out, norm_post_mix[l])
        h_lat = _modulate(_rms(x_lat, norm_pre_ffn[l]), m_lat[3], m_lat[4])
        x_lat = x_lat + m_lat[5] * _rms(_swiglu(h_lat, ffn_w_gate[l], ffn_w_up[l], ffn_w_down[l]), norm_post_ffn[l])
        if not last:
            x_ctx = x_ctx + m_ctx[2] * _rms(o_ctx @ w_out, norm_post_mix[l])
            h_ctx = _modulate(_rms(x_ctx, norm_pre_ffn[l]), m_ctx[3], m_ctx[4])
            x_ctx = x_ctx + m_ctx[5] * _rms(_swiglu(h_ctx, ffn_w_gate[l], ffn_w_up[l], ffn_w_down[l]), norm_post_ffn[l])
    return x_lat
```

```python
import functools

import numpy as np
import jax
import jax.numpy as jnp
from jax import lax
from jax.experimental import pallas as pl
from jax.experimental.pallas import tpu as pltpu

GRID_W = 64
CHUNK = 64
GLA_HEADS = 4
GLA_DK = 64
GLA_DV = 128
GLA_GATE_RANK = 16
GLA_GATE_TAU = 16.0
ATT_Q_HEADS = 8
ATT_KV_HEADS = 2
ATT_HEAD_DIM = 64
ATT_GROUP = ATT_Q_HEADS // ATT_KV_HEADS
ROPE_THETA = 10000.0
HGRN_HEADS = 8
HGRN_DF = 128
EPS = 1e-6

LANES = 128
VMEM_LIMIT_BYTES = 60000 * 1024

ROW_TILE = 256
N_LEVELS = 6

F32 = jnp.float32
BF16 = jnp.bfloat16


def _dot(a, b):
    return jnp.dot(a, b, preferred_element_type=F32)


def _dot_nt(a, b):
    return lax.dot_general(a, b, (((1,), (1,)), ((), ())), preferred_element_type=F32)


def _dot_tn(a, b):
    return lax.dot_general(a, b, (((0,), (0,)), ((), ())), preferred_element_type=F32)


def _rms_rows(x, gain):
    return x * lax.rsqrt(jnp.mean(x * x, axis=-1, keepdims=True) + EPS) * gain


def _silu(x):
    return x / (1.0 + jnp.exp(-x))


def _log_sigmoid(x):
    return jnp.minimum(x, 0.0) - jnp.log1p(jnp.exp(-jnp.abs(x)))


def _params(*sem):
    return pltpu.CompilerParams(dimension_semantics=sem, vmem_limit_bytes=VMEM_LIMIT_BYTES)


def _const_spec(shape):
    nd = len(shape)
    return pl.BlockSpec(shape, lambda *_: (0,) * nd, pipeline_mode=pl.Buffered(1))


def _mod_kernel(s_ref, w_ref, b_ref, o_ref):
    s = _silu(s_ref[...]).astype(BF16)
    o_ref[0] = _dot(s, w_ref[0].astype(BF16)) + b_ref[0]


def _modulation(cond, mod_w, mod_b):
    L, D, D6 = mod_w.shape
    R = cond.shape[0]
    tn = 2048
    return pl.pallas_call(
        _mod_kernel,
        grid=(L, D6 // tn),
        in_specs=[
            pl.BlockSpec((R, D), lambda l, j: (0, 0)),
            pl.BlockSpec((1, D, tn), lambda l, j: (l, 0, j)),
            pl.BlockSpec((1, 1, tn), lambda l, j: (l, 0, j)),
        ],
        out_specs=pl.BlockSpec((1, R, tn), lambda l, j: (l, 0, j)),
        out_shape=jax.ShapeDtypeStruct((L, R, D6), F32),
        compiler_params=_params("arbitrary", "arbitrary"),
        name="modulation",
    )(cond, mod_w, mod_b.reshape(L, 1, D6))


def _modulated_norm(x, mod_row, gain, shift_idx, D):
    shift = mod_row[:, shift_idx * D:(shift_idx + 1) * D]
    scale = mod_row[:, (shift_idx + 1) * D:(shift_idx + 2) * D]
    return _rms_rows(x, gain) * (1.0 + scale) + shift


E_GQ, E_GK, E_GV, E_GOG, E_AQ, E_AK, E_AV, E_GZ = 0, 256, 512, 1024, 1536, 2048, 2176, 2304
E_END = 2336


def _swap_halves(x):
    w = x.shape[-1]
    half = ATT_HEAD_DIM // 2
    lane = lax.broadcasted_iota(jnp.int32, x.shape, x.ndim - 1)
    first_half = (lane % ATT_HEAD_DIM) < half
    return jnp.where(first_half, pltpu.roll(x, w - half, x.ndim - 1), pltpu.roll(x, half, x.ndim - 1))


def _head_norm_rope(y, bd_ref, gain_ref, cs, sn):
    reps = y.shape[-1] // LANES
    ms = _dot((y * y).astype(BF16), bd_ref[...])
    y = y * lax.rsqrt(ms + EPS) * gain_ref[...]
    cs = jnp.concatenate([cs] * reps, axis=1) if reps > 1 else cs
    sn = jnp.concatenate([sn] * reps, axis=1) if reps > 1 else sn
    return y * cs + _swap_halves(y) * sn


def _pre_even_kernel(x_ref, mod_ref, gain_ref, w_ref, wg_ref, bg_ref, qn_ref, kn_ref, bdq_ref, bdk_ref,
                     cs_ref, sn_ref,
                     gq_ref, gk_ref, gv_ref, gog_ref, gf_ref, gb_ref, aq_ref, ak_ref, av_ref):
    D = x_ref.shape[-1]
    h = _modulated_norm(x_ref[0], mod_ref[0], gain_ref[...], 0, D).astype(BF16)
    acc = _dot(h, w_ref[...])
    gq_ref[0] = (acc[:, E_GQ:E_GK] * (GLA_DK ** -0.5)).astype(BF16)
    gk_ref[0] = acc[:, E_GK:E_GV].astype(BF16)
    gv_ref[0] = acc[:, E_GV:E_GOG].astype(BF16)
    gog_ref[0] = acc[:, E_GOG:E_AQ].astype(BF16)
    z = acc[:, E_GZ:E_END].astype(BF16)
    g = _log_sigmoid(_dot(z, wg_ref[...]) + bg_ref[...]) * (1.0 / GLA_GATE_TAU)
    nk = GLA_HEADS * GLA_DK
    gf_ref[0] = g[:, :nk]
    gb_ref[0] = g[:, nk:]
    cs = cs_ref[...]
    sn = sn_ref[...]
    aq = _head_norm_rope(acc[:, E_AQ:E_AK], bdq_ref, qn_ref, cs, sn)
    aq_ref[0] = (aq * (ATT_HEAD_DIM ** -0.5)).astype(BF16)
    ak_ref[0] = _head_norm_rope(acc[:, E_AK:E_AV], bdk_ref, kn_ref, cs, sn).astype(BF16)
    av_ref[0] = acc[:, E_AV:E_GZ].astype(BF16)


def _pre_even(xs, mods, mod_base, gain, w, wg, bg, qn, kn, bdq, bdk, cs, sn, n_ctx):
    B, N, D = xs.shape
    tm = ROW_TILE
    nct = n_ctx // tm
    ctx_row = mods.shape[0] // 2 - 1

    def mod_map(b, i):
        return (mod_base + jnp.where(i < nct, B, b), 0, 0)

    row = lambda w_: pl.BlockSpec((1, tm, w_), lambda b, i: (b, i, 0))
    widths = (256, 256, 512, 512, 256, 256, 512, 128, 128)
    dtypes = (BF16, BF16, BF16, BF16, F32, F32, BF16, BF16, BF16)
    del ctx_row
    return pl.pallas_call(
        _pre_even_kernel,
        grid=(B, N // tm),
        in_specs=[
            row(D),
            pl.BlockSpec((1, 1, mods.shape[-1]), mod_map),
            _const_spec(gain.shape),
            _const_spec(w.shape),
            _const_spec(wg.shape),
            _const_spec(bg.shape),
            _const_spec(qn.shape),
            _const_spec(kn.shape),
            _const_spec(bdq.shape),
            _const_spec(bdk.shape),
            pl.BlockSpec((tm, LANES), lambda b, i: (i, 0)),
            pl.BlockSpec((tm, LANES), lambda b, i: (i, 0)),
        ],
        out_specs=[row(w_) for w_ in widths],
        out_shape=[jax.ShapeDtypeStruct((B, N, w_), dt) for w_, dt in zip(widths, dtypes)],
        compiler_params=_params("arbitrary", "arbitrary"),
        name="pre_even",
    )(xs, mods, gain, w, wg, bg, qn, kn, bdq, bdk, cs, sn)


def _pre_odd_kernel(x_ref, mod_ref, gain_ref, w_ref, lb_ref,
                    q_ref, kf_ref, kb_ref, gf_ref, gb_ref, v_ref, og_ref, *, layer):
    D = x_ref.shape[-1]
    F = q_ref.shape[-1]
    h = _modulated_norm(x_ref[0], mod_ref[0], gain_ref[...], 0, D).astype(BF16)
    acc = _dot(h, w_ref[...])
    q_ref[0] = _silu(acc[:, 0:F]).astype(BF16)
    v_ref[0] = acc[:, 3 * F:4 * F].astype(BF16)
    og_ref[0] = acc[:, 4 * F:5 * F].astype(BF16)
    for d, (k_ref, g_ref) in enumerate(((kf_ref, gf_ref), (kb_ref, gb_ref))):
        raw = lb_ref[d]
        p = jnp.exp(raw - jnp.max(raw, axis=0, keepdims=True))
        p = p / jnp.sum(p, axis=0, keepdims=True)
        lb = jnp.sum(p[1:layer + 1], axis=0, keepdims=True) if layer > 0 else jnp.zeros_like(p[0:1])
        x = acc[:, (1 + d) * F:(2 + d) * F]
        t = jnp.exp(-jnp.abs(x))
        r = 1.0 / (1.0 + t)
        sig_pos = jnp.where(x >= 0, r, t * r)
        sig_neg = jnp.where(x >= 0, t * r, r)
        g_ref[0] = jnp.log(lb + (1.0 - lb) * sig_pos)
        k_ref[0] = ((1.0 - lb) * sig_neg).astype(BF16)


def _pre_odd(xs, mods, mod_base, gain, w, lower_bounds, layer, n_ctx):
    B, N, D = xs.shape
    F = w.shape[1] // 5
    tm = ROW_TILE
    nct = n_ctx // tm

    def mod_map(b, i):
        return (mod_base + jnp.where(i < nct, B, b), 0, 0)

    row = lambda w_: pl.BlockSpec((1, tm, w_), lambda b, i: (b, i, 0))
    dtypes = (BF16, BF16, BF16, F32, F32, BF16, BF16)
    return pl.pallas_call(
        functools.partial(_pre_odd_kernel, layer=layer),
        grid=(B, N // tm),
        in_specs=[
            row(D),
            pl.BlockSpec((1, 1, mods.shape[-1]), mod_map),
            _const_spec(gain.shape),
            _const_spec(w.shape),
            _const_spec(lower_bounds.shape),
        ],
        out_specs=[row(F) for _ in dtypes],
        out_shape=[jax.ShapeDtypeStruct((B, N, F), dt) for dt in dtypes],
        compiler_params=_params("arbitrary", "arbitrary"),
        name="pre_odd",
    )(xs, mods, gain, w, lower_bounds)


def _scan_constants(reverse, hpg):
    C = CHUNK
    K = LANES // hpg
    i = np.arange(C)[:, None]
    t = np.arange(C)[None, :]
    lane_head = np.arange(LANES)[None, :] // K
    blocks = [(t >= i) if reverse else (t <= i)]
    mq, mk, pm = [], [], []
    size = C
    while size >= 2:
        half = size // 2
        a = (i // size) * size + half
        a_t = (t // size) * size + half
        same = (i // size) == (t // size)
        if reverse:
            qside = i < a
            blocks.append(np.where(qside, (t >= i) & (t < a), (t >= a) & (t < i)))
            pair = same & (i < a) & (t >= a_t)
        else:
            qside = i >= a
            blocks.append(np.where(qside, (t > a) & (t <= i), (t > i) & (t <= a)))
            pair = same & (i >= a) & (t < a_t)
        for hh in range(hpg):
            mq.append(qside & (lane_head == hh))
            pm.append(pair)
        mk.append(np.broadcast_to(~qside, (C, LANES)))
        size = half
    blocks.append((t < i) if reverse else (t > i))
    dmat = np.concatenate(blocks, axis=0).astype(np.float32)
    return (dmat, np.concatenate(mq, 0).astype(np.float32), np.concatenate(mk, 0).astype(np.float32),
            np.concatenate(pm, 0).astype(np.float32))


def _scan_kernel(q_ref, kf_ref, kb_ref, gf_ref, gb_ref, v_ref, og_ref, gain_ref,
                 dmat_ref, mq_ref, mk_ref, pm_ref,
                 o_ref, of_scr, ob_scr, st_scr, *, hpg, n_chunks, n_ctx_chunks, out_start):
    C = CHUNK
    K = LANES // hpg
    DV = LANES

    st_scr[...] = jnp.zeros_like(st_scr)
    lane = lax.broadcasted_iota(jnp.int32, (C, LANES), 1)
    st_row = lax.broadcasted_iota(jnp.int32, (hpg * DV, LANES), 0)
    st_lane = lax.broadcasted_iota(jnp.int32, (hpg * DV, LANES), 1)
    st_mask = (st_row // DV) == (st_lane // K)

    def chunk_step(d, r0, k_ref, g_ref, o_scr):
        q = q_ref[0, pl.ds(r0, C), :].astype(F32)
        k = k_ref[0, pl.ds(r0, C), :].astype(F32)
        g = g_ref[0, pl.ds(r0, C), :]
        v = v_ref[0, pl.ds(r0, C), :]
        g_hi = g.astype(BF16)
        g_lo = (g - g_hi.astype(F32)).astype(BF16)
        r = _dot(dmat_ref[d], jnp.concatenate([g_hi, g_lo], axis=1))
        r = r[:, :LANES] + r[:, LANES:]
        cum = r[0:C]
        a = jnp.zeros((hpg * C, C), F32)
        for l in range(N_LEVELS):
            e = jnp.exp(r[(l + 1) * C:(l + 2) * C])
            kl = (k * (e * mk_ref[d, l * C:(l + 1) * C, :])).astype(BF16)
            ql = [(q * (e * mq_ref[d, (l * hpg + hh) * C:(l * hpg + hh + 1) * C, :])).astype(BF16)
                  for hh in range(hpg)]
            ql = jnp.concatenate(ql, axis=0) if hpg > 1 else ql[0]
            a = a + _dot_nt(ql, kl) * pm_ref[d, l * hpg * C:(l + 1) * hpg * C, :]
        qk = q * k
        st = st_scr[d]
        inter = _dot_nt((q * jnp.exp(cum)).astype(BF16), st.astype(BF16))
        outs = []
        for hh in range(hpg):
            v_h = v[:, hh * DV:(hh + 1) * DV]
            diag = jnp.sum(jnp.where(lane // K == hh, qk, 0.0), axis=1, keepdims=True)
            outs.append(_dot(a[hh * C:(hh + 1) * C].astype(BF16), v_h) + diag * v_h.astype(F32)
                        + inter[:, hh * DV:(hh + 1) * DV])
        o_scr[pl.ds(r0, C), :] = jnp.concatenate(outs, axis=1) if hpg > 1 else outs[0]
        ks = (k * jnp.exp(r[(N_LEVELS + 1) * C:(N_LEVELS + 2) * C])).astype(BF16)
        edge = cum[0:1] if d == 1 else cum[C - 1:C]
        upd = _dot_tn(v, ks)
        if hpg > 1:
            upd = jnp.where(st_mask, upd, 0.0)
        st_scr[d] = st * jnp.exp(edge) + upd

    def body(t, carry):
        rf = pl.multiple_of(t * C, C)
        cb = jnp.where(t < n_ctx_chunks, n_ctx_chunks - 1 - t, n_chunks - 1 - (t - n_ctx_chunks))
        rb = pl.multiple_of(cb * C, C)
        chunk_step(0, rf, kf_ref, gf_ref, of_scr)
        chunk_step(1, rb, kb_ref, gb_ref, ob_scr)
        return carry

    lax.fori_loop(0, n_chunks, body, 0)

    def readout(c, carry):
        r0 = pl.multiple_of(c * C, C)
        o = of_scr[pl.ds(r0, C), :] + ob_scr[pl.ds(r0, C), :]
        og = og_ref[0, pl.ds(r0, C), :].astype(F32)
        gate = _silu(og)
        outs = [_rms_rows(o[:, hh * DV:(hh + 1) * DV], gain_ref[...]) * gate[:, hh * DV:(hh + 1) * DV]
                for hh in range(hpg)]
        y = jnp.concatenate(outs, axis=1) if hpg > 1 else outs[0]
        w0 = pl.multiple_of((c - out_start) * C, C)
        o_ref[0, pl.ds(w0, C), :] = y.astype(o_ref.dtype)
        return carry

    lax.fori_loop(out_start, n_chunks, readout, 0)


def _scan(q, kf, kb, gf, gb, v, og, gain, *, hpg, n_ctx, ctx_out):
    B, N, HK = q.shape
    n_groups = HK // LANES
    n_chunks = N // CHUNK
    n_ctx_chunks = n_ctx // CHUNK
    out_start = 0 if ctx_out else n_ctx_chunks
    n_out = N - out_start * CHUNK
    wv = hpg * LANES
    consts = [np.stack(c) for c in zip(_scan_constants(False, hpg), _scan_constants(True, hpg))]
    dmat = jnp.asarray(consts[0], BF16)
    mq, mk, pm = (jnp.asarray(c, F32) for c in consts[1:])
    seq = lambda w_: pl.BlockSpec((1, N, w_), lambda b, j: (b, 0, j))
    kern = functools.partial(_scan_kernel, hpg=hpg, n_chunks=n_chunks, n_ctx_chunks=n_ctx_chunks,
                             out_start=out_start)
    return pl.pallas_call(
        kern,
        grid=(B, n_groups),
        in_specs=[seq(LANES), seq(LANES), seq(LANES), seq(LANES), seq(LANES), seq(wv), seq(wv),
                  _const_spec(gain.shape), _const_spec(dmat.shape), _const_spec(mq.shape),
                  _const_spec(mk.shape), _const_spec(pm.shape)],
        out_specs=pl.BlockSpec((1, n_out, wv), lambda b, j: (b, 0, j)),
        out_shape=jax.ShapeDtypeStruct((B, n_out, n_groups * wv), BF16),
        scratch_shapes=[pltpu.VMEM((N, wv), F32), pltpu.VMEM((N, wv), F32),
                        pltpu.VMEM((2, wv, LANES), F32)],
        compiler_params=_params("arbitrary", "arbitrary"),
        name="scan_hpg%d" % hpg,
    )(q, kf, kb, gf, gb, v, og, gain, dmat, mq, mk, pm)


def _attend(q2, k2, v2):
    lane = lax.broadcasted_iota(jnp.int32, q2.shape, 1)
    outs = []
    for kv in range(ATT_KV_HEADS):
        qm = jnp.where(lane // ATT_HEAD_DIM == kv, q2, jnp.zeros_like(q2))
        s = _dot_nt(qm, k2)
        p = jnp.exp(s - jnp.max(s, axis=-1, keepdims=True))
        denom = jnp.sum(p, axis=-1, keepdims=True)
        outs.append(_dot(p.astype(BF16), v2) / denom)
    return jnp.where(lane // ATT_HEAD_DIM == 0, outs[0], outs[1])


def _attn_kernel(q_ref, k_ref, v_ref, o_ref, *, n_ctx, n_ctx_tiles):
    i = pl.program_id(1)

    @pl.when(i < n_ctx_tiles)
    def _():
        o_ref[0] = _attend(q_ref[0], k_ref[0, :n_ctx, :], v_ref[0, :n_ctx, :]).astype(o_ref.dtype)

    @pl.when(i >= n_ctx_tiles)
    def _():
        o_ref[0] = _attend(q_ref[0], k_ref[0], v_ref[0]).astype(o_ref.dtype)


def _attention(aq, ak, av, n_ctx):
    B, N, W = aq.shape
    tq = ROW_TILE
    kern = functools.partial(_attn_kernel, n_ctx=n_ctx, n_ctx_tiles=n_ctx // tq)
    return pl.pallas_call(
        kern,
        grid=(B, N // tq, W // LANES),
        in_specs=[
            pl.BlockSpec((1, tq, LANES), lambda b, i, j: (b, i, j)),
            pl.BlockSpec((1, N, LANES), lambda b, i, j: (b, 0, 0)),
            pl.BlockSpec((1, N, LANES), lambda b, i, j: (b, 0, 0)),
        ],
        out_specs=pl.BlockSpec((1, tq, LANES), lambda b, i, j: (b, i, j)),
        out_shape=jax.ShapeDtypeStruct((B, N, W), BF16),
        compiler_params=_params("arbitrary", "arbitrary", "arbitrary"),
        name="attention",
    )(aq, ak, av)


def _post_kernel(*refs, n_mix):
    x_ref, mod_ref = refs[0], refs[1]
    mix_refs = refs[2:2 + n_mix]
    wout_refs = refs[2 + n_mix:2 + 2 * n_mix]
    g_pm_ref, g_pf_ref, g_pff_ref, wg_ref, wu_ref, wd_ref, o_ref = refs[2 + 2 * n_mix:]
    D = x_ref.shape[-1]
    m = mod_ref[0]
    y = _dot(mix_refs[0][0], wout_refs[0][...])
    for mr, wr in zip(mix_refs[1:], wout_refs[1:]):
        y = y + _dot(mr[0], wr[...])
    x1 = x_ref[0] + m[:, 2 * D:3 * D] * _rms_rows(y, g_pm_ref[...])
    h = _modulated_norm(x1, m, g_pf_ref[...], 3, D).astype(BF16)
    act = (_silu(_dot(h, wg_ref[...])) * _dot(h, wu_ref[...])).astype(BF16)
    z = _dot(act, wd_ref[...])
    o_ref[0] = x1 + m[:, 5 * D:6 * D] * _rms_rows(z, g_pff_ref[...])


def _post(xs, mods, mod_base, mixes, wouts, g_pm, g_pf, g_pff, wg, wu, wd, *, n_ctx, row_start):
    B, N, D = xs.shape
    tm = ROW_TILE
    nct = n_ctx // tm
    t0 = row_start // tm
    n_rows = N - row_start

    def mod_map(b, i):
        return (mod_base + jnp.where(i + t0 < nct, B, b), 0, 0)

    in_specs = [pl.BlockSpec((1, tm, D), lambda b, i: (b, i + t0, 0)),
                pl.BlockSpec((1, 1, mods.shape[-1]), mod_map)]
    in_specs += [pl.BlockSpec((1, tm, m.shape[-1]), lambda b, i: (b, i, 0)) for m in mixes]
    in_specs += [_const_spec(a.shape) for a in (*wouts, g_pm, g_pf, g_pff, wg, wu, wd)]
    return pl.pallas_call(
        functools.partial(_post_kernel, n_mix=len(mixes)),
        grid=(B, n_rows // tm),
        in_specs=in_specs,
        out_specs=pl.BlockSpec((1, tm, D), lambda b, i: (b, i, 0)),
        out_shape=jax.ShapeDtypeStruct((B, n_rows, D), F32),
        compiler_params=_params("arbitrary", "arbitrary"),
        name="post",
    )(xs, mods, *mixes, *wouts, g_pm, g_pf, g_pff, wg, wu, wd)


def _deinterleave(n_heads):
    within = np.concatenate([np.arange(0, ATT_HEAD_DIM, 2), np.arange(1, ATT_HEAD_DIM, 2)])
    return (np.arange(n_heads)[:, None] * ATT_HEAD_DIM + within[None, :]).reshape(-1)


def _attn_head_order():
    return np.array([kv * ATT_GROUP + j for j in range(ATT_GROUP) for kv in range(ATT_KV_HEADS)])


def _rope_tables(n_ctx, T):
    rows_n = T // GRID_W
    row = jnp.repeat(jnp.arange(rows_n), GRID_W).astype(F32)
    col = jnp.tile(jnp.arange(GRID_W), rows_n).astype(F32)
    axis_dim = ATT_HEAD_DIM // 2
    inv = ROPE_THETA ** (-jnp.arange(0, axis_dim, 2, dtype=F32) / axis_dim)
    ang = jnp.concatenate([row[:, None] * inv, col[:, None] * inv], axis=-1)
    cos = jnp.concatenate([jnp.ones((n_ctx, axis_dim), F32), jnp.cos(ang)], axis=0)
    sin = jnp.concatenate([jnp.zeros((n_ctx, axis_dim), F32), jnp.sin(ang)], axis=0)
    reps = LANES // ATT_HEAD_DIM
    cs = jnp.tile(jnp.concatenate([cos, cos], axis=1), (1, reps))
    sn = jnp.tile(jnp.concatenate([-sin, sin], axis=1), (1, reps))
    return cs, sn


def _block_mean_matrix(width, group):
    idx = np.arange(width) // group
    return jnp.asarray((idx[:, None] == idx[None, :]).astype(np.float32) / group, BF16)


def kernel(x, c, ctx, c_ctx, mod_w, mod_b, norm_pre_mix, norm_post_mix, norm_pre_ffn, norm_post_ffn, even_w_in,
           gla_w_gate, gla_b_gate, gla_out_norm, att_q_norm, att_k_norm, even_w_out, odd_w_in, hgrn_lower_bounds,
           hgrn_out_norm, odd_w_out, ffn_w_gate, ffn_w_up, ffn_w_down):
    B, T, D = x.shape
    n_ctx = ctx.shape[1]
    depth = mod_w.shape[0]
    assert n_ctx % ROW_TILE == 0 and T % ROW_TILE == 0 and T % GRID_W == 0 and D % LANES == 0

    n_rows = -(-(B + 1) // 8) * 8
    cond = jnp.concatenate([c, c_ctx[None, :], jnp.zeros((n_rows - B - 1, D), F32)], axis=0)
    mods = _modulation(cond, mod_w, mod_b).reshape(depth * n_rows, 1, 6 * D)

    xs = jnp.concatenate([ctx, x], axis=1)
    vec = lambda a: a.reshape(1, -1).astype(F32)
    bf = lambda a: a.astype(BF16)
    cs, sn = _rope_tables(n_ctx, T)

    for l in range(depth):
        last = l == depth - 1
        j = l // 2
        mod_base = l * n_rows
        if l % 2 == 0:
            sizes = np.cumsum([0, GLA_HEADS * GLA_DK, GLA_HEADS * GLA_DK, GLA_HEADS * GLA_DV, GLA_HEADS * GLA_DV,
                               2 * GLA_GATE_RANK, ATT_Q_HEADS * ATT_HEAD_DIM, ATT_KV_HEADS * ATT_HEAD_DIM,
                               ATT_KV_HEADS * ATT_HEAD_DIM])
            o_gq, o_gk, o_gv, o_gog, o_gz, o_aq, o_ak, o_av = sizes[:-1]
            q_cols = o_aq + (_attn_head_order()[:, None] * ATT_HEAD_DIM
                             + _deinterleave(1)[None, :]).reshape(-1)
            k_cols = o_ak + _deinterleave(ATT_KV_HEADS)
            cols = np.concatenate([np.arange(o_gq, o_gz), q_cols, k_cols, np.arange(o_av, sizes[-1]),
                                   np.arange(o_gz, o_aq)])
            w_in = bf(even_w_in[j][:, cols])
            wgate = gla_w_gate[j]
            nk = GLA_HEADS * GLA_DK
            zeros = jnp.zeros((GLA_GATE_RANK, nk), F32)
            wg2 = bf(jnp.concatenate([jnp.concatenate([wgate[0], zeros], axis=1),
                                      jnp.concatenate([zeros, wgate[1]], axis=1)], axis=0))
            bg2 = gla_b_gate[j].reshape(1, 2 * nk).astype(F32)
            qn = vec(jnp.tile(att_q_norm[j][_deinterleave(1)], ATT_Q_HEADS))
            kn = vec(jnp.tile(att_k_norm[j][_deinterleave(1)], ATT_KV_HEADS))
            bdq = _block_mean_matrix(ATT_Q_HEADS * ATT_HEAD_DIM, ATT_HEAD_DIM)
            bdk = _block_mean_matrix(ATT_KV_HEADS * ATT_HEAD_DIM, ATT_HEAD_DIM)
            gq, gk, gv, gog, gf, gb, aq, ak, av = _pre_even(
                xs, mods, mod_base, vec(norm_pre_mix[l]), w_in, wg2, bg2, qn, kn, bdq, bdk, cs, sn, n_ctx)
            mix_a = _scan(gq, gk, gk, gf, gb, gv, gog, vec(gla_out_norm[j]), hpg=LANES // GLA_DK, n_ctx=n_ctx,
                          ctx_out=not last)
            mix_b = _attention(aq, ak, av, n_ctx)
            if last:
                mix_b = mix_b[:, n_ctx:]
            n_a = GLA_HEADS * GLA_DV
            b_rows = n_a + (_attn_head_order()[:, None] * ATT_HEAD_DIM
                            + np.arange(ATT_HEAD_DIM)[None, :]).reshape(-1)
            mixes = (mix_a, mix_b)
            wouts = (bf(even_w_out[j][:n_a]), bf(even_w_out[j][b_rows]))
        else:
            q, kf, kb, gf, gb, v, og = _pre_odd(xs, mods, mod_base, vec(norm_pre_mix[l]), bf(odd_w_in[j]),
                                                hgrn_lower_bounds.astype(F32), l, n_ctx)
            mixes = (_scan(q, kf, kb, gf, gb, v, og, vec(hgrn_out_norm[j]), hpg=LANES // HGRN_DF, n_ctx=n_ctx,
                           ctx_out=not last),)
            wouts = (bf(odd_w_out[j]),)
        new_rows = _post(xs, mods, mod_base, mixes, wouts, vec(norm_post_mix[l]), vec(norm_pre_ffn[l]),
                         vec(norm_post_ffn[l]), bf(ffn_w_gate[l]), bf(ffn_w_up[l]), bf(ffn_w_down[l]),
                         n_ctx=n_ctx, row_start=n_ctx if last else 0)
        xs = new_rows
    return xs
```

```python
import functools

import numpy as np
import jax
import jax.numpy as jnp
from jax import lax
from jax.experimental import pallas as pl
from jax.experimental.pallas import tpu as pltpu

GRID_W = 64
GLA_HEADS = 4
GLA_DK = 64
GLA_DV = 128
GLA_GATE_RANK = 16
GLA_GATE_TAU = 16.0
ATT_Q_HEADS = 8
ATT_KV_HEADS = 2
ATT_HEAD_DIM = 64
ATT_GROUP = ATT_Q_HEADS // ATT_KV_HEADS
ROPE_THETA = 10000.0
HGRN_HEADS = 8
HGRN_DF = 128
EPS = 1e-6

LANES = 128
VMEM_LIMIT_BYTES = 60000 * 1024

ROW_TILE = 256
CHUNK = 128
N_LEVELS = 7
SCAN_UNROLL = 2

F32 = jnp.float32
BF16 = jnp.bfloat16
LOG2E = 1.4426950408889634


def _dot(a, b):
    return jnp.dot(a, b, preferred_element_type=F32)


def _dot_nt(a, b):
    return lax.dot_general(a, b, (((1,), (1,)), ((), ())), preferred_element_type=F32)


def _dot_tn(a, b):
    return lax.dot_general(a, b, (((0,), (0,)), ((), ())), preferred_element_type=F32)


def _rms_rows(x, gain):
    return x * lax.rsqrt(jnp.mean(x * x, axis=-1, keepdims=True) + EPS) * gain


def _silu(x):
    return x / (1.0 + jnp.exp(-x))


def _neg_abs(x):
    bits = lax.bitcast_convert_type(x, jnp.uint32) | jnp.uint32(0x80000000)
    return lax.bitcast_convert_type(bits, F32)


def _log_sigmoid(x):
    return jnp.minimum(x, 0.0) - jnp.log1p(jnp.exp(-jnp.abs(x)))


def _params(*sem):
    return pltpu.CompilerParams(dimension_semantics=sem, vmem_limit_bytes=VMEM_LIMIT_BYTES)


def _const_spec(shape):
    nd = len(shape)
    return pl.BlockSpec(shape, lambda *_: (0,) * nd, pipeline_mode=pl.Buffered(1))


def _mod_kernel(s_ref, w_ref, b_ref, o_ref):
    s = _silu(s_ref[...]).astype(BF16)
    o_ref[0] = _dot(s, w_ref[0].astype(BF16)) + b_ref[0]


def _modulation(cond, mod_w, mod_b):
    L, D, D6 = mod_w.shape
    R = cond.shape[0]
    tn = 2048
    return pl.pallas_call(
        _mod_kernel,
        grid=(L, D6 // tn),
        in_specs=[
            pl.BlockSpec((R, D), lambda l, j: (0, 0)),
            pl.BlockSpec((1, D, tn), lambda l, j: (l, 0, j)),
            pl.BlockSpec((1, 1, tn), lambda l, j: (l, 0, j)),
        ],
        out_specs=pl.BlockSpec((1, R, tn), lambda l, j: (l, 0, j)),
        out_shape=jax.ShapeDtypeStruct((L, R, D6), F32),
        compiler_params=_params("arbitrary", "arbitrary"),
        name="modulation",
    )(cond, mod_w, mod_b.reshape(L, 1, D6))


def _modulated_norm(x, mod_row, gain, shift_idx, D):
    shift = mod_row[:, shift_idx * D:(shift_idx + 1) * D]
    scale = mod_row[:, (shift_idx + 1) * D:(shift_idx + 2) * D]
    return _rms_rows(x, gain) * (1.0 + scale) + shift


E_GQ, E_GK, E_GV, E_GOG, E_AQ, E_AK, E_AV, E_GZ = 0, 256, 512, 1024, 1536, 2048, 2176, 2304
E_END = 2336


def _swap_halves(x):
    w = x.shape[-1]
    half = ATT_HEAD_DIM // 2
    lane = lax.broadcasted_iota(jnp.int32, x.shape, x.ndim - 1)
    first_half = (lane % ATT_HEAD_DIM) < half
    return jnp.where(first_half, pltpu.roll(x, w - half, x.ndim - 1), pltpu.roll(x, half, x.ndim - 1))


def _head_norm_rope(y, bd_ref, gain_ref, cs, sn):
    reps = y.shape[-1] // LANES
    ms = _dot((y * y).astype(BF16), bd_ref[...])
    y = y * lax.rsqrt(ms + EPS) * gain_ref[...]
    cs = jnp.concatenate([cs] * reps, axis=1) if reps > 1 else cs
    sn = jnp.concatenate([sn] * reps, axis=1) if reps > 1 else sn
    return y * cs + _swap_halves(y) * sn


def _pre_even_kernel(x_ref, mod_ref, gain_ref, w_ref, wg_ref, bg_ref, qn_ref, kn_ref, bdq_ref, bdk_ref,
                     cs_ref, sn_ref,
                     gq_ref, gk_ref, gv_ref, gog_ref, gf_ref, gb_ref, aq_ref, ak_ref, av_ref):
    D = x_ref.shape[-1]
    h = _modulated_norm(x_ref[0], mod_ref[0], gain_ref[...], 0, D).astype(BF16)
    acc = _dot(h, w_ref[...])
    gq_ref[0] = (acc[:, E_GQ:E_GK] * (GLA_DK ** -0.5)).astype(BF16)
    gk_ref[0] = acc[:, E_GK:E_GV].astype(BF16)
    gv_ref[0] = acc[:, E_GV:E_GOG].astype(BF16)
    gog_ref[0] = acc[:, E_GOG:E_AQ].astype(BF16)
    z = acc[:, E_GZ:E_END].astype(BF16)
    g = _log_sigmoid(_dot(z, wg_ref[...]) + bg_ref[...]) * (1.0 / GLA_GATE_TAU)
    nk = GLA_HEADS * GLA_DK
    gf_ref[0] = g[:, :nk]
    gb_ref[0] = g[:, nk:]
    cs = cs_ref[...]
    sn = sn_ref[...]
    aq = _head_norm_rope(acc[:, E_AQ:E_AK], bdq_ref, qn_ref, cs, sn)
    aq_ref[0] = (aq * (ATT_HEAD_DIM ** -0.5 * LOG2E)).astype(BF16)
    ak_ref[0] = _head_norm_rope(acc[:, E_AK:E_AV], bdk_ref, kn_ref, cs, sn).astype(BF16)
    av = acc[:, E_AV:E_GZ]
    first = lax.broadcasted_iota(jnp.int32, av.shape, 1) < ATT_HEAD_DIM
    av_ref[0] = jnp.concatenate([jnp.where(first, av, 1.0), jnp.where(first, 1.0, av)], axis=1).astype(BF16)


def _pre_even(xs, mods, mod_base, gain, w, wg, bg, qn, kn, bdq, bdk, cs, sn, n_ctx):
    B, N, D = xs.shape
    tm = ROW_TILE
    nct = n_ctx // tm
    ctx_row = mods.shape[0] // 2 - 1

    def mod_map(b, i):
        return (mod_base + jnp.where(i < nct, B, b), 0, 0)

    row = lambda w_: pl.BlockSpec((1, tm, w_), lambda b, i: (b, i, 0))
    widths = (256, 256, 512, 512, 256, 256, 512, 128, 256)
    dtypes = (BF16, BF16, BF16, BF16, F32, F32, BF16, BF16, BF16)
    del ctx_row
    return pl.pallas_call(
        _pre_even_kernel,
        grid=(B, N // tm),
        in_specs=[
            row(D),
            pl.BlockSpec((1, 1, mods.shape[-1]), mod_map),
            _const_spec(gain.shape),
            _const_spec(w.shape),
            _const_spec(wg.shape),
            _const_spec(bg.shape),
            _const_spec(qn.shape),
            _const_spec(kn.shape),
            _const_spec(bdq.shape),
            _const_spec(bdk.shape),
            pl.BlockSpec((tm, LANES), lambda b, i: (i, 0)),
            pl.BlockSpec((tm, LANES), lambda b, i: (i, 0)),
        ],
        out_specs=[row(w_) for w_ in widths],
        out_shape=[jax.ShapeDtypeStruct((B, N, w_), dt) for w_, dt in zip(widths, dtypes)],
        compiler_params=_params("arbitrary", "arbitrary"),
        name="pre_even",
    )(xs, mods, gain, w, wg, bg, qn, kn, bdq, bdk, cs, sn)


def _pre_odd_kernel(x_ref, mod_ref, gain_ref, w_ref, lb_ref,
                    q_ref, kf_ref, kb_ref, gf_ref, gb_ref, v_ref, og_ref, *, layer):
    D = x_ref.shape[-1]
    F = q_ref.shape[-1]
    h = _modulated_norm(x_ref[0], mod_ref[0], gain_ref[...], 0, D).astype(BF16)
    acc = _dot(h, w_ref[...])
    q_ref[0] = _silu(acc[:, 0:F]).astype(BF16)
    v_ref[0] = acc[:, 3 * F:4 * F].astype(BF16)
    og_ref[0] = acc[:, 4 * F:5 * F].astype(BF16)
    for d, (k_ref, g_ref) in enumerate(((kf_ref, gf_ref), (kb_ref, gb_ref))):
        raw = lb_ref[d]
        p = jnp.exp(raw - jnp.max(raw, axis=0, keepdims=True))
        p = p / jnp.sum(p, axis=0, keepdims=True)
        lb = jnp.sum(p[1:layer + 1], axis=0, keepdims=True) if layer > 0 else jnp.zeros_like(p[0:1])
        x = acc[:, (1 + d) * F:(2 + d) * F]
        t = jnp.exp(-jnp.abs(x))
        r = 1.0 / (1.0 + t)
        sig_pos = jnp.where(x >= 0, r, t * r)
        sig_neg = jnp.where(x >= 0, t * r, r)
        g_ref[0] = jnp.log(lb + (1.0 - lb) * sig_pos)
        k_ref[0] = ((1.0 - lb) * sig_neg).astype(BF16)


def _pre_odd(xs, mods, mod_base, gain, w, lower_bounds, layer, n_ctx):
    B, N, D = xs.shape
    F = w.shape[1] // 5
    tm = ROW_TILE
    nct = n_ctx // tm

    def mod_map(b, i):
        return (mod_base + jnp.where(i < nct, B, b), 0, 0)

    row = lambda w_: pl.BlockSpec((1, tm, w_), lambda b, i: (b, i, 0))
    dtypes = (BF16, BF16, BF16, F32, F32, BF16, BF16)
    return pl.pallas_call(
        functools.partial(_pre_odd_kernel, layer=layer),
        grid=(B, N // tm),
        in_specs=[
            row(D),
            pl.BlockSpec((1, 1, mods.shape[-1]), mod_map),
            _const_spec(gain.shape),
            _const_spec(w.shape),
            _const_spec(lower_bounds.shape),
        ],
        out_specs=[row(F) for _ in dtypes],
        out_shape=[jax.ShapeDtypeStruct((B, N, F), dt) for dt in dtypes],
        compiler_params=_params("arbitrary", "arbitrary"),
        name="pre_odd",
    )(xs, mods, gain, w, lower_bounds)


def _scan_constants(reverse):
    C = CHUNK
    i = np.arange(C)[:, None]
    t = np.arange(C)[None, :]
    tri = (t >= i) if reverse else (t <= i)
    pm = []
    size = C
    while size >= 2:
        half = size // 2
        a = (i // size) * size + half
        a_t = (t // size) * size + half
        same = (i // size) == (t // size)
        pm.append(same & ((i < a) & (t >= a_t) if reverse else (i >= a) & (t < a_t)))
        size = half
    return tri.astype(np.float32), np.concatenate(pm, 0).astype(np.float32)


def _anchor_rows(G, size):
    C = G.shape[0]
    half = size // 2
    if size >= 8:
        parts = [jnp.broadcast_to(G[b * size + half:b * size + half + 1, :], (size, G.shape[1]))
                 for b in range(C // size)]
        return jnp.concatenate(parts, axis=0) if len(parts) > 1 else parts[0]
    row = lax.broadcasted_iota(jnp.int32, G.shape, 0) % size
    out = G
    for r in range(size):
        if r != half:
            out = jnp.where(row == r, pltpu.roll(G, (C - (half - r)) % C, 0), out)
    return out


def _scan_kernel(q_ref, kf_ref, kb_ref, gf_ref, gb_ref, v_ref, og_ref, gain_ref,
                 tri_ref, pm_ref,
                 o_ref, of_scr, ob_scr, st_scr, *, hpg, n_chunks, n_ctx_chunks, out_start, unroll):
    C = CHUNK
    K = LANES // hpg
    DV = LANES

    st_scr[...] = jnp.zeros_like(st_scr)
    lane = lax.broadcasted_iota(jnp.int32, (C, LANES), 1)
    row = lax.broadcasted_iota(jnp.int32, (C, LANES), 0)
    st_row = lax.broadcasted_iota(jnp.int32, (hpg * DV, LANES), 0)
    st_lane = lax.broadcasted_iota(jnp.int32, (hpg * DV, LANES), 1)
    st_mask = (st_row // DV) == (st_lane // K)

    k_refs, g_refs, o_scrs = (kf_ref, kb_ref), (gf_ref, gb_ref), (of_scr, ob_scr)

    def load(d, r0):
        g = g_refs[d][0, pl.ds(r0, C), :] * LOG2E
        g_hi = g.astype(BF16)
        g_lo = (g - g_hi.astype(F32)).astype(BF16)
        r = _dot(tri_ref[d], jnp.concatenate([g_hi, g_lo], axis=1))
        return dict(d=d, r0=r0, cum=r[:, :LANES] + r[:, LANES:],
                    q=q_ref[0, pl.ds(r0, C), :].astype(F32), k=k_refs[d][0, pl.ds(r0, C), :].astype(F32),
                    v=v_ref[0, pl.ds(r0, C), :])

    def level_scores(it, l):
        d = it["d"]
        cum = it["cum"]
        half = (C >> l) // 2
        later_half = (row & half) != 0
        qside = jnp.logical_not(later_half) if d == 1 else later_half
        e = jnp.exp2(_neg_abs(cum - _anchor_rows(cum, C >> l)))
        t = jnp.where(qside, it["q"], it["k"]) * e
        pm = pm_ref[d, l * C:(l + 1) * C, :]
        if hpg == 1:
            return _dot_nt(t.astype(BF16), t.astype(BF16)) * pm
        lhs = jnp.concatenate([jnp.where(lane // K == hh, t, 0.0) for hh in range(hpg)], axis=0)
        return _dot_nt(lhs.astype(BF16), t.astype(BF16)) * jnp.concatenate([pm] * hpg, axis=0)

    def finish(it, a, st):
        d, cum, q, k, v = it["d"], it["cum"], it["q"], it["k"], it["v"]
        edge = cum[0:1] if d == 1 else cum[C - 1:C]
        qk = q * k
        inter = _dot_nt((q * jnp.exp2(cum)).astype(BF16), st.astype(BF16))
        outs = []
        for hh in range(hpg):
            v_h = v[:, hh * DV:(hh + 1) * DV]
            diag = jnp.sum(jnp.where(lane // K == hh, qk, 0.0), axis=1, keepdims=True)
            outs.append(_dot(a[hh * C:(hh + 1) * C].astype(BF16), v_h) + diag * v_h.astype(F32)
                        + inter[:, hh * DV:(hh + 1) * DV])
        o_scrs[d][pl.ds(it["r0"], C), :] = jnp.concatenate(outs, axis=1) if hpg > 1 else outs[0]
        ks = (k * jnp.exp2(edge - cum)).astype(BF16)
        upd = _dot_tn(v, ks)
        if hpg > 1:
            upd = jnp.where(st_mask, upd, 0.0)
        return st * jnp.exp2(edge) + upd

    def body(t, carry):
        items = []
        for u in range(unroll):
            tf = t * unroll + u
            cb = jnp.where(tf < n_ctx_chunks, n_ctx_chunks - 1 - tf, n_chunks - 1 - (tf - n_ctx_chunks))
            items.append(load(0, pl.multiple_of(tf * C, C)))
            items.append(load(1, pl.multiple_of(cb * C, C)))
        accs = [jnp.zeros((hpg * C, C), F32) for _ in items]
        for l in range(N_LEVELS):
            accs = [a + level_scores(it, l) for it, a in zip(items, accs)]
        for d in range(2):
            st = st_scr[d]
            for it, a in zip(items, accs):
                if it["d"] == d:
                    st = finish(it, a, st)
            st_scr[d] = st
        return carry

    lax.fori_loop(0, n_chunks // unroll, body, 0)

    def readout(c, carry):
        r0 = pl.multiple_of(c * C, C)
        o = of_scr[pl.ds(r0, C), :] + ob_scr[pl.ds(r0, C), :]
        og = og_ref[0, pl.ds(r0, C), :].astype(F32)
        gate = _silu(og)
        outs = [_rms_rows(o[:, hh * DV:(hh + 1) * DV], gain_ref[...]) * gate[:, hh * DV:(hh + 1) * DV]
                for hh in range(hpg)]
        y = jnp.concatenate(outs, axis=1) if hpg > 1 else outs[0]
        w0 = pl.multiple_of((c - out_start) * C, C)
        o_ref[0, pl.ds(w0, C), :] = y.astype(o_ref.dtype)
        return carry

    lax.fori_loop(out_start, n_chunks, readout, 0)


def _scan(q, kf, kb, gf, gb, v, og, gain, *, hpg, n_ctx, ctx_out):
    B, N, HK = q.shape
    n_groups = HK // LANES
    n_chunks = N // CHUNK
    n_ctx_chunks = n_ctx // CHUNK
    out_start = 0 if ctx_out else n_ctx_chunks
    n_out = N - out_start * CHUNK
    wv = hpg * LANES
    tri, pm = (np.stack(c) for c in zip(_scan_constants(False), _scan_constants(True)))
    tri = jnp.asarray(tri, BF16)
    pm = jnp.asarray(pm, F32)
    seq = lambda w_: pl.BlockSpec((1, N, w_), lambda b, j: (b, 0, j))
    unroll = SCAN_UNROLL
    assert n_ctx_chunks % unroll == 0 and n_chunks % unroll == 0
    kern = functools.partial(_scan_kernel, hpg=hpg, n_chunks=n_chunks, n_ctx_chunks=n_ctx_chunks,
                             out_start=out_start, unroll=unroll)
    return pl.pallas_call(
        kern,
        grid=(B, n_groups),
        in_specs=[seq(LANES), seq(LANES), seq(LANES), seq(LANES), seq(LANES), seq(wv), seq(wv),
                  _const_spec(gain.shape), _const_spec(tri.shape), _const_spec(pm.shape)],
        out_specs=pl.BlockSpec((1, n_out, wv), lambda b, j: (b, 0, j)),
        out_shape=jax.ShapeDtypeStruct((B, n_out, n_groups * wv), BF16),
        scratch_shapes=[pltpu.VMEM((N, wv), F32), pltpu.VMEM((N, wv), F32),
                        pltpu.VMEM((2, wv, LANES), F32)],
        compiler_params=_params("arbitrary", "arbitrary"),
        name="scan_hpg%d" % hpg,
    )(q, kf, kb, gf, gb, v, og, gain, tri, pm)


def _attend(q2, k2, v2):
    lane = lax.broadcasted_iota(jnp.int32, q2.shape, 1)
    outs = []
    for kv in range(ATT_KV_HEADS):
        qm = jnp.where(lane // ATT_HEAD_DIM == kv, q2, jnp.zeros_like(q2))
        s = _dot_nt(qm, k2)
        p = jnp.exp2((s - jnp.max(s, axis=-1, keepdims=True)).astype(BF16))
        pv = _dot(p, v2[:, kv * LANES:(kv + 1) * LANES])
        outs.append(pv / pltpu.roll(pv, ATT_HEAD_DIM, 1))
    return jnp.where(lane // ATT_HEAD_DIM == 0, outs[0], outs[1])


def _attn_kernel(q_ref, k_ref, v_ref, o_ref, *, n_ctx, n_ctx_tiles):
    i = pl.program_id(1)

    @pl.when(i < n_ctx_tiles)
    def _():
        o_ref[0] = _attend(q_ref[0], k_ref[0, :n_ctx, :], v_ref[0, :n_ctx, :]).astype(o_ref.dtype)

    @pl.when(i >= n_ctx_tiles)
    def _():
        o_ref[0] = _attend(q_ref[0], k_ref[0], v_ref[0]).astype(o_ref.dtype)


def _attention(aq, ak, av, n_ctx):
    B, N, W = aq.shape
    tq = ROW_TILE
    kern = functools.partial(_attn_kernel, n_ctx=n_ctx, n_ctx_tiles=n_ctx // tq)
    return pl.pallas_call(
        kern,
        grid=(B, N // tq, W // LANES),
        in_specs=[
            pl.BlockSpec((1, tq, LANES), lambda b, i, j: (b, i, j)),
            pl.BlockSpec((1, N, LANES), lambda b, i, j: (b, 0, 0)),
            pl.BlockSpec((1, N, ATT_KV_HEADS * LANES), lambda b, i, j: (b, 0, 0)),
        ],
        out_specs=pl.BlockSpec((1, tq, LANES), lambda b, i, j: (b, i, j)),
        out_shape=jax.ShapeDtypeStruct((B, N, W), BF16),
        compiler_params=_params("arbitrary", "arbitrary", "arbitrary"),
        name="attention",
    )(aq, ak, av)


def _post_kernel(*refs, n_mix):
    x_ref, mod_ref = refs[0], refs[1]
    mix_refs = refs[2:2 + n_mix]
    wout_refs = refs[2 + n_mix:2 + 2 * n_mix]
    g_pm_ref, g_pf_ref, g_pff_ref, wg_ref, wu_ref, wd_ref, o_ref = refs[2 + 2 * n_mix:]
    D = x_ref.shape[-1]
    m = mod_ref[0]
    y = _dot(mix_refs[0][0], wout_refs[0][...])
    for mr, wr in zip(mix_refs[1:], wout_refs[1:]):
        y = y + _dot(mr[0], wr[...])
    x1 = x_ref[0] + m[:, 2 * D:3 * D] * _rms_rows(y, g_pm_ref[...])
    h = _modulated_norm(x1, m, g_pf_ref[...], 3, D).astype(BF16)
    act = (_silu(_dot(h, wg_ref[...])) * _dot(h, wu_ref[...])).astype(BF16)
    z = _dot(act, wd_ref[...])
    o_ref[0] = x1 + m[:, 5 * D:6 * D] * _rms_rows(z, g_pff_ref[...])


def _post(xs, mods, mod_base, mixes, wouts, g_pm, g_pf, g_pff, wg, wu, wd, *, n_ctx, row_start):
    B, N, D = xs.shape
    tm = ROW_TILE
    nct = n_ctx // tm
    t0 = row_start // tm
    n_rows = N - row_start

    def mod_map(b, i):
        return (mod_base + jnp.where(i + t0 < nct, B, b), 0, 0)

    in_specs = [pl.BlockSpec((1, tm, D), lambda b, i: (b, i + t0, 0)),
                pl.BlockSpec((1, 1, mods.shape[-1]), mod_map)]
    in_specs += [pl.BlockSpec((1, tm, m.shape[-1]), lambda b, i: (b, i, 0)) for m in mixes]
    in_specs += [_const_spec(a.shape) for a in (*wouts, g_pm, g_pf, g_pff, wg, wu, wd)]
    return pl.pallas_call(
        functools.partial(_post_kernel, n_mix=len(mixes)),
        grid=(B, n_rows // tm),
        in_specs=in_specs,
        out_specs=pl.BlockSpec((1, tm, D), lambda b, i: (b, i, 0)),
        out_shape=jax.ShapeDtypeStruct((B, n_rows, D), F32),
        compiler_params=_params("arbitrary", "arbitrary"),
        name="post",
    )(xs, mods, *mixes, *wouts, g_pm, g_pf, g_pff, wg, wu, wd)


def _deinterleave(n_heads):
    within = np.concatenate([np.arange(0, ATT_HEAD_DIM, 2), np.arange(1, ATT_HEAD_DIM, 2)])
    return (np.arange(n_heads)[:, None] * ATT_HEAD_DIM + within[None, :]).reshape(-1)


def _attn_head_order():
    return np.array([kv * ATT_GROUP + j for j in range(ATT_GROUP) for kv in range(ATT_KV_HEADS)])


def _rope_tables(n_ctx, T):
    rows_n = T // GRID_W
    row = jnp.repeat(jnp.arange(rows_n), GRID_W).astype(F32)
    col = jnp.tile(jnp.arange(GRID_W), rows_n).astype(F32)
    axis_dim = ATT_HEAD_DIM // 2
    inv = ROPE_THETA ** (-jnp.arange(0, axis_dim, 2, dtype=F32) / axis_dim)
    ang = jnp.concatenate([row[:, None] * inv, col[:, None] * inv], axis=-1)
    cos = jnp.concatenate([jnp.ones((n_ctx, axis_dim), F32), jnp.cos(ang)], axis=0)
    sin = jnp.concatenate([jnp.zeros((n_ctx, axis_dim), F32), jnp.sin(ang)], axis=0)
    reps = LANES // ATT_HEAD_DIM
    cs = jnp.tile(jnp.concatenate([cos, cos], axis=1), (1, reps))
    sn = jnp.tile(jnp.concatenate([-sin, sin], axis=1), (1, reps))
    return cs, sn


def _block_mean_matrix(width, group):
    idx = np.arange(width) // group
    return jnp.asarray((idx[:, None] == idx[None, :]).astype(np.float32) / group, BF16)


def kernel(x, c, ctx, c_ctx, mod_w, mod_b, norm_pre_mix, norm_post_mix, norm_pre_ffn, norm_post_ffn, even_w_in,
           gla_w_gate, gla_b_gate, gla_out_norm, att_q_norm, att_k_norm, even_w_out, odd_w_in, hgrn_lower_bounds,
           hgrn_out_norm, odd_w_out, ffn_w_gate, ffn_w_up, ffn_w_down):
    B, T, D = x.shape
    n_ctx = ctx.shape[1]
    depth = mod_w.shape[0]
    assert n_ctx % ROW_TILE == 0 and T % ROW_TILE == 0 and T % GRID_W == 0 and D % LANES == 0

    n_rows = -(-(B + 1) // 8) * 8
    cond = jnp.concatenate([c, c_ctx[None, :], jnp.zeros((n_rows - B - 1, D), F32)], axis=0)
    mods = _modulation(cond, mod_w, mod_b).reshape(depth * n_rows, 1, 6 * D)

    xs = jnp.concatenate([ctx, x], axis=1)
    vec = lambda a: a.reshape(1, -1).astype(F32)
    bf = lambda a: a.astype(BF16)
    cs, sn = _rope_tables(n_ctx, T)

    for l in range(depth):
        last = l == depth - 1
        j = l // 2
        mod_base = l * n_rows
        if l % 2 == 0:
            sizes = np.cumsum([0, GLA_HEADS * GLA_DK, GLA_HEADS * GLA_DK, GLA_HEADS * GLA_DV, GLA_HEADS * GLA_DV,
                               2 * GLA_GATE_RANK, ATT_Q_HEADS * ATT_HEAD_DIM, ATT_KV_HEADS * ATT_HEAD_DIM,
                               ATT_KV_HEADS * ATT_HEAD_DIM])
            o_gq, o_gk, o_gv, o_gog, o_gz, o_aq, o_ak, o_av = sizes[:-1]
            q_cols = o_aq + (_attn_head_order()[:, None] * ATT_HEAD_DIM
                             + _deinterleave(1)[None, :]).reshape(-1)
            k_cols = o_ak + _deinterleave(ATT_KV_HEADS)
            cols = np.concatenate([np.arange(o_gq, o_gz), q_cols, k_cols, np.arange(o_av, sizes[-1]),
                                   np.arange(o_gz, o_aq)])
            w_in = bf(even_w_in[j][:, cols])
            wgate = gla_w_gate[j]
            nk = GLA_HEADS * GLA_DK
            zeros = jnp.zeros((GLA_GATE_RANK, nk), F32)
            wg2 = bf(jnp.concatenate([jnp.concatenate([wgate[0], zeros], axis=1),
                                      jnp.concatenate([zeros, wgate[1]], axis=1)], axis=0))
            bg2 = gla_b_gate[j].reshape(1, 2 * nk).astype(F32)
            qn = vec(jnp.tile(att_q_norm[j][_deinterleave(1)], ATT_Q_HEADS))
            kn = vec(jnp.tile(att_k_norm[j][_deinterleave(1)], ATT_KV_HEADS))
            bdq = _block_mean_matrix(ATT_Q_HEADS * ATT_HEAD_DIM, ATT_HEAD_DIM)
            bdk = _block_mean_matrix(ATT_KV_HEADS * ATT_HEAD_DIM, ATT_HEAD_DIM)
            gq, gk, gv, gog, gf, gb, aq, ak, av = _pre_even(
                xs, mods, mod_base, vec(norm_pre_mix[l]), w_in, wg2, bg2, qn, kn, bdq, bdk, cs, sn, n_ctx)
            mix_a = _scan(gq, gk, gk, gf, gb, gv, gog, vec(gla_out_norm[j]), hpg=LANES // GLA_DK, n_ctx=n_ctx,
                          ctx_out=not last)
            mix_b = _attention(aq, ak, av, n_ctx)
            if last:
                mix_b = mix_b[:, n_ctx:]
            n_a = GLA_HEADS * GLA_DV
            b_rows = n_a + (_attn_head_order()[:, None] * ATT_HEAD_DIM
                            + np.arange(ATT_HEAD_DIM)[None, :]).reshape(-1)
            mixes = (mix_a, mix_b)
            wouts = (bf(even_w_out[j][:n_a]), bf(even_w_out[j][b_rows]))
        else:
            q, kf, kb, gf, gb, v, og = _pre_odd(xs, mods, mod_base, vec(norm_pre_mix[l]), bf(odd_w_in[j]),
                                                hgrn_lower_bounds.astype(F32), l, n_ctx)
            mixes = (_scan(q, kf, kb, gf, gb, v, og, vec(hgrn_out_norm[j]), hpg=LANES // HGRN_DF, n_ctx=n_ctx,
                           ctx_out=not last),)
            wouts = (bf(odd_w_out[j]),)
        new_rows = _post(xs, mods, mod_base, mixes, wouts, vec(norm_post_mix[l]), vec(norm_pre_ffn[l]),
                         vec(norm_post_ffn[l]), bf(ffn_w_gate[l]), bf(ffn_w_up[l]), bf(ffn_w_down[l]),
                         n_ctx=n_ctx, row_start=n_ctx if last else 0)
        xs = new_rows
    return xs
```

```python
import functools

import numpy as np
import jax
import jax.numpy as jnp
from jax import lax
from jax.experimental import pallas as pl
from jax.experimental.pallas import tpu as pltpu

GRID_W = 64
GLA_HEADS = 4
GLA_DK = 64
GLA_DV = 128
GLA_GATE_RANK = 16
GLA_GATE_TAU = 16.0
ATT_Q_HEADS = 8
ATT_KV_HEADS = 2
ATT_HEAD_DIM = 64
ATT_GROUP = ATT_Q_HEADS // ATT_KV_HEADS
ROPE_THETA = 10000.0
HGRN_HEADS = 8
HGRN_DF = 128
EPS = 1e-6

LANES = 128
VMEM_LIMIT_BYTES = 60000 * 1024

LAT_ROW_TILE = 512
CTX_ROW_TILE = 256
SUB_ROWS = 256
ATT_Q_TILE = 256
CHUNK = 128
N_LEVELS = 7
SCAN_UNROLL = 4
READOUT_UNROLL = 4
F32 = jnp.float32
BF16 = jnp.bfloat16
LOG2E = 1.4426950408889634


def _dot(a, b):
    return jnp.dot(a, b, preferred_element_type=F32)


def _dot_nt(a, b):
    return lax.dot_general(a, b, (((1,), (1,)), ((), ())), preferred_element_type=F32)


def _dot_tn(a, b):
    return lax.dot_general(a, b, (((0,), (0,)), ((), ())), preferred_element_type=F32)


def _rms_rows(x, gain):
    return x * lax.rsqrt(jnp.mean(x * x, axis=-1, keepdims=True) + EPS) * gain


def _silu(x):
    return x / (1.0 + jnp.exp(-x))


def _log_sigmoid(x):
    return jnp.minimum(x, 0.0) - jnp.log1p(jnp.exp(-jnp.abs(x)))


def _params(*sem):
    return pltpu.CompilerParams(dimension_semantics=sem, vmem_limit_bytes=VMEM_LIMIT_BYTES)


def _const_spec(shape):
    nd = len(shape)
    return pl.BlockSpec(shape, lambda *_: (0,) * nd, pipeline_mode=pl.Buffered(1))


def _sub_tiles(n_rows):
    sub = min(SUB_ROWS, n_rows)
    return [slice(s, s + sub) for s in range(0, n_rows, sub)]


def _mod_kernel(s_ref, w_ref, b_ref, o_ref):
    s = _silu(s_ref[...]).astype(BF16)
    o_ref[0] = _dot(s, w_ref[0].astype(BF16)) + b_ref[0]


def _modulation(cond, mod_w, mod_b):
    L, D, D6 = mod_w.shape
    R = cond.shape[0]
    tn = 2048
    return pl.pallas_call(
        _mod_kernel,
        grid=(L, D6 // tn),
        in_specs=[
            pl.BlockSpec((R, D), lambda l, j: (0, 0)),
            pl.BlockSpec((1, D, tn), lambda l, j: (l, 0, j)),
            pl.BlockSpec((1, 1, tn), lambda l, j: (l, 0, j)),
        ],
        out_specs=pl.BlockSpec((1, R, tn), lambda l, j: (l, 0, j)),
        out_shape=jax.ShapeDtypeStruct((L, R, D6), F32),
        compiler_params=_params("arbitrary", "arbitrary"),
        name="modulation",
    )(cond, mod_w, mod_b.reshape(L, 1, D6))


def _modulated_norm(x, mod_row, gain, shift_idx, D):
    shift = mod_row[:, shift_idx * D:(shift_idx + 1) * D]
    scale = mod_row[:, (shift_idx + 1) * D:(shift_idx + 2) * D]
    return _rms_rows(x, gain) * (1.0 + scale) + shift


def _row_call(kernel, name, x, mods, mod_row, consts, row_inputs, out_widths, out_dtypes, tm):
    B, R, D = x.shape
    assert R % tm == 0
    row = lambda w_: pl.BlockSpec((1, tm, w_), lambda b, i: (b, i, 0))
    in_specs = [row(D), pl.BlockSpec((1, 1, mods.shape[-1]), lambda b, i: (mod_row(b), 0, 0))]
    in_specs += [pl.BlockSpec((tm, a.shape[-1]), lambda b, i: (i, 0)) if a.ndim == 2 else row(a.shape[-1])
                 for a in row_inputs]
    in_specs += [_const_spec(a.shape) for a in consts]
    return pl.pallas_call(
        kernel,
        grid=(B, R // tm),
        in_specs=in_specs,
        out_specs=[row(w_) for w_ in out_widths],
        out_shape=[jax.ShapeDtypeStruct((B, R, w_), dt) for w_, dt in zip(out_widths, out_dtypes)],
        compiler_params=_params("arbitrary", "arbitrary"),
        name=name,
    )(x, mods, *row_inputs, *consts)


E_GQ, E_GK, E_GV, E_GOG, E_AQ, E_AK, E_AV, E_GZ, E_END = 0, 256, 512, 1024, 1536, 2048, 2176, 2304, 2336


def _swap_halves(x):
    w = x.shape[-1]
    half = ATT_HEAD_DIM // 2
    lane = lax.broadcasted_iota(jnp.int32, x.shape, x.ndim - 1)
    first_half = (lane % ATT_HEAD_DIM) < half
    return jnp.where(first_half, pltpu.roll(x, w - half, x.ndim - 1), pltpu.roll(x, half, x.ndim - 1))


def _head_norm(y, bd_ref, gain_ref):
    ms = _dot((y * y).astype(BF16), bd_ref[...])
    return y * lax.rsqrt(ms + EPS) * gain_ref[...]


def _rope(y, cs, sn):
    reps = y.shape[-1] // LANES
    cs = jnp.concatenate([cs] * reps, axis=1) if reps > 1 else cs
    sn = jnp.concatenate([sn] * reps, axis=1) if reps > 1 else sn
    return y * cs + _swap_halves(y) * sn


def _pre_even_kernel(*refs, rope):
    x_ref, mod_ref = refs[:2]
    refs = refs[2:]
    if rope:
        cs_ref, sn_ref = refs[:2]
        refs = refs[2:]
    gain_ref, w_ref, wg_ref, bg_ref, qn_ref, kn_ref, bdq_ref, bdk_ref = refs[:8]
    gq_ref, gk_ref, gv_ref, gog_ref, gf_ref, gb_ref, aq_ref, ak_ref, av_ref = refs[8:]
    D = x_ref.shape[-1]
    nk = GLA_HEADS * GLA_DK
    subs = _sub_tiles(x_ref.shape[1])
    hs = [_modulated_norm(x_ref[0, sl, :], mod_ref[0], gain_ref[...], 0, D).astype(BF16) for sl in subs]
    for sl, h in zip(subs, hs):
        acc = _dot(h, w_ref[:, E_GQ:E_GV])
        gq_ref[0, sl, :] = (acc[:, :nk] * (GLA_DK ** -0.5)).astype(BF16)
        gk_ref[0, sl, :] = acc[:, nk:].astype(BF16)
        gv_ref[0, sl, :] = _dot(h, w_ref[:, E_GV:E_GOG]).astype(BF16)
        gog_ref[0, sl, :] = _dot(h, w_ref[:, E_GOG:E_AQ]).astype(BF16)
        aq = _head_norm(_dot(h, w_ref[:, E_AQ:E_AK]), bdq_ref, qn_ref)
        acc = _dot(h, w_ref[:, E_AK:E_END])
        ak = _head_norm(acc[:, :E_AV - E_AK], bdk_ref, kn_ref)
        if rope:
            aq = _rope(aq, cs_ref[sl, :], sn_ref[sl, :])
            ak = _rope(ak, cs_ref[sl, :], sn_ref[sl, :])
        aq_ref[0, sl, :] = (aq * (ATT_HEAD_DIM ** -0.5 * LOG2E)).astype(BF16)
        ak_ref[0, sl, :] = ak.astype(BF16)
        av = acc[:, E_AV - E_AK:E_GZ - E_AK]
        first = lax.broadcasted_iota(jnp.int32, av.shape, 1) < ATT_HEAD_DIM
        av_ref[0, sl, :] = jnp.concatenate([jnp.where(first, av, 1.0), jnp.where(first, 1.0, av)],
                                           axis=1).astype(BF16)
        z = acc[:, E_GZ - E_AK:].astype(BF16)
        g = _log_sigmoid(_dot(z, wg_ref[...]) + bg_ref[...]) * (1.0 / GLA_GATE_TAU)
        gf_ref[0, sl, :] = g[:, :nk]
        gb_ref[0, sl, :] = g[:, nk:]


def _pre_even(x, mods, mod_row, consts, rope_tables, tm):
    widths = (256, 256, 512, 512, 256, 256, 512, 128, 256)
    dtypes = (BF16, BF16, BF16, BF16, F32, F32, BF16, BF16, BF16)
    rope = rope_tables is not None
    return _row_call(functools.partial(_pre_even_kernel, rope=rope), "pre_even_lat" if rope else "pre_even_ctx",
                     x, mods, mod_row, consts, rope_tables if rope else (), widths, dtypes, tm)


def _pre_odd_kernel(x_ref, mod_ref, gain_ref, w_ref, lb_ref,
                    q_ref, kf_ref, kb_ref, gf_ref, gb_ref, v_ref, og_ref, *, layer):
    D = x_ref.shape[-1]
    F = q_ref.shape[-1]
    subs = _sub_tiles(x_ref.shape[1])
    hs = [_modulated_norm(x_ref[0, sl, :], mod_ref[0], gain_ref[...], 0, D).astype(BF16) for sl in subs]
    lbs = []
    for d in range(2):
        raw = lb_ref[d]
        p = jnp.exp(raw - jnp.max(raw, axis=0, keepdims=True))
        p = p / jnp.sum(p, axis=0, keepdims=True)
        lbs.append(jnp.sum(p[1:layer + 1], axis=0, keepdims=True) if layer > 0 else jnp.zeros_like(p[0:1]))
    for sl, h in zip(subs, hs):
        q_ref[0, sl, :] = _silu(_dot(h, w_ref[:, 0:F])).astype(BF16)
        for d, (k_ref, g_ref) in enumerate(((kf_ref, gf_ref), (kb_ref, gb_ref))):
            lb = lbs[d]
            x = _dot(h, w_ref[:, (1 + d) * F:(2 + d) * F])
            t = jnp.exp(-jnp.abs(x))
            r = 1.0 / (1.0 + t)
            sig_pos = jnp.where(x >= 0, r, t * r)
            sig_neg = jnp.where(x >= 0, t * r, r)
            g_ref[0, sl, :] = jnp.log(lb + (1.0 - lb) * sig_pos)
            k_ref[0, sl, :] = ((1.0 - lb) * sig_neg).astype(BF16)
        v_ref[0, sl, :] = _dot(h, w_ref[:, 3 * F:4 * F]).astype(BF16)
        og_ref[0, sl, :] = _dot(h, w_ref[:, 4 * F:5 * F]).astype(BF16)


def _pre_odd(x, mods, mod_row, gain, w, lower_bounds, layer, tm, name):
    F = w.shape[1] // 5
    dtypes = (BF16, BF16, BF16, F32, F32, BF16, BF16)
    return _row_call(functools.partial(_pre_odd_kernel, layer=layer), name, x, mods, mod_row,
                     (gain, w, lower_bounds), (), (F,) * len(dtypes), dtypes, tm)


def _scan_constants(reverse):
    C = CHUNK
    i = np.arange(C)[:, None]
    t = np.arange(C)[None, :]
    tri = (t >= i) if reverse else (t <= i)
    pm = []
    size = C
    while size >= 2:
        half = size // 2
        a = (i // size) * size + half
        a_t = (t // size) * size + half
        same = (i // size) == (t // size)
        pm.append(same & ((i < a) & (t >= a_t) if reverse else (i >= a) & (t < a_t)))
        size = half
    return tri.astype(np.float32), np.concatenate(pm, 0).astype(np.float32)


def _anchor_rows(G, size):
    C = G.shape[0]
    half = size // 2
    if size >= 8:
        parts = [jnp.broadcast_to(G[b * size + half:b * size + half + 1, :], (size, G.shape[1]))
                 for b in range(C // size)]
        return jnp.concatenate(parts, axis=0) if len(parts) > 1 else parts[0]
    row = lax.broadcasted_iota(jnp.int32, G.shape, 0) % size
    out = G
    for r in range(size):
        if r != half:
            out = jnp.where(row == r, pltpu.roll(G, (C - (half - r)) % C, 0), out)
    return out


def _scan_kernel(*refs, hpg, n_lat_chunks, n_ctx_chunks, ctx_out, unroll):
    lat_refs, ctx_refs = refs[0:7], refs[7:14]
    gain_ref, tri_ref, pm_ref = refs[14:17]
    n_out = 2 if ctx_out else 1
    out_refs = refs[17:17 + n_out]
    of_lat, ob_lat, of_ctx, ob_ctx, st_scr = refs[17 + n_out:]
    C = CHUNK
    K = LANES // hpg
    DV = LANES

    st_scr[...] = jnp.zeros_like(st_scr)
    lane = lax.broadcasted_iota(jnp.int32, (C, LANES), 1)
    row = lax.broadcasted_iota(jnp.int32, (C, LANES), 0)
    st_row = lax.broadcasted_iota(jnp.int32, (hpg * DV, LANES), 0)
    st_lane = lax.broadcasted_iota(jnp.int32, (hpg * DV, LANES), 1)
    st_mask = (st_row // DV) == (st_lane // K)

    def load(seg, d, r0):
        q_ref, v_ref = seg[0], seg[5]
        k_ref, g_ref = seg[1 + d], seg[3 + d]
        g = g_ref[0, pl.ds(r0, C), :] * LOG2E
        g_hi = g.astype(BF16)
        g_lo = (g - g_hi.astype(F32)).astype(BF16)
        r = _dot(tri_ref[d], jnp.concatenate([g_hi, g_lo], axis=1))
        return dict(d=d, r0=r0, cum=r[:, :LANES] + r[:, LANES:],
                    q=q_ref[0, pl.ds(r0, C), :].astype(F32), k=k_ref[0, pl.ds(r0, C), :].astype(F32),
                    v=v_ref[0, pl.ds(r0, C), :])

    def level_scores(it, l):
        d, cum = it["d"], it["cum"]
        half = (C >> l) // 2
        later_half = (row & half) != 0
        qside = jnp.logical_not(later_half) if d == 1 else later_half
        e = jnp.exp2(-jnp.abs(cum - _anchor_rows(cum, C >> l)))
        t = jnp.where(qside, it["q"], it["k"]) * e
        pm = pm_ref[d, l * C:(l + 1) * C, :]
        if hpg == 1:
            return _dot_nt(t.astype(BF16), t.astype(BF16)) * pm
        lhs = jnp.concatenate([jnp.where(lane // K == hh, t, 0.0) for hh in range(hpg)], axis=0)
        return _dot_nt(lhs.astype(BF16), t.astype(BF16)) * jnp.concatenate([pm] * hpg, axis=0)

    def finish(it, acc, st, o_scr):
        d, cum, q, k, v = it["d"], it["cum"], it["q"], it["k"], it["v"]
        edge = cum[0:1] if d == 1 else cum[C - 1:C]
        qk = q * k
        inter = _dot_nt((q * jnp.exp2(cum)).astype(BF16), st.astype(BF16))
        outs = []
        for hh in range(hpg):
            v_h = v[:, hh * DV:(hh + 1) * DV]
            diag = jnp.sum(qk if hpg == 1 else jnp.where(lane // K == hh, qk, 0.0), axis=1, keepdims=True)
            a_h = acc[hh * C:(hh + 1) * C].astype(BF16)
            outs.append(_dot(a_h, v_h) + diag * v_h.astype(F32) + inter[:, hh * DV:(hh + 1) * DV])
        o_scr[pl.ds(it["r0"], C), :] = jnp.concatenate(outs, axis=1) if hpg > 1 else outs[0]
        ks = (k * jnp.exp2(edge - cum)).astype(BF16)
        upd = _dot_tn(v, ks)
        if hpg > 1:
            upd = jnp.where(st_mask, upd, 0.0)
        return st * jnp.exp2(edge) + upd

    def trip(seg, o_scrs, n_seg, t, n_u):
        items = []
        for u in range(n_u):
            cf = t * n_u + u
            items.append(load(seg, 0, cf * C if isinstance(cf, int) else pl.multiple_of(cf * C, C)))
            cb = n_seg - 1 - cf
            items.append(load(seg, 1, cb * C if isinstance(cb, int) else pl.multiple_of(cb * C, C)))
        sts = [st_scr[0], st_scr[1]]
        for u in range(n_u):
            pair = items[2 * u:2 * u + 2]
            accs = [level_scores(it, 0) for it in pair]
            for l in range(1, N_LEVELS):
                accs = [acc + level_scores(it, l) for it, acc in zip(pair, accs)]
            for it, acc in zip(pair, accs):
                sts[it["d"]] = finish(it, acc, sts[it["d"]], o_scrs[it["d"]])
        st_scr[0] = sts[0]
        st_scr[1] = sts[1]

    def run_segment(seg, o_scrs, n_seg):
        n_u = min(unroll, n_seg)
        n_trips = n_seg // n_u
        if n_trips == 1:
            trip(seg, o_scrs, n_seg, 0, n_u)
        else:
            def body(t, carry):
                trip(seg, o_scrs, n_seg, t, n_u)
                return carry
            lax.fori_loop(0, n_trips, body, 0)

    def readout_chunks(o_scrs, og_ref, out_ref, c0, n):
        for u in range(n):
            c = c0 + u
            r0 = c * C if isinstance(c, int) else pl.multiple_of(c * C, C)
            o = o_scrs[0][pl.ds(r0, C), :] + o_scrs[1][pl.ds(r0, C), :]
            gate = _silu(og_ref[0, pl.ds(r0, C), :].astype(F32))
            outs = [_rms_rows(o[:, hh * DV:(hh + 1) * DV], gain_ref[...]) * gate[:, hh * DV:(hh + 1) * DV]
                    for hh in range(hpg)]
            y = jnp.concatenate(outs, axis=1) if hpg > 1 else outs[0]
            out_ref[0, pl.ds(r0, C), :] = y.astype(out_ref.dtype)

    def readout(o_scrs, og_ref, out_ref, n_seg):
        ru = min(READOUT_UNROLL, n_seg)
        if n_seg == ru:
            readout_chunks(o_scrs, og_ref, out_ref, 0, ru)
        else:
            def body(t, carry):
                readout_chunks(o_scrs, og_ref, out_ref, t * ru, ru)
                return carry
            lax.fori_loop(0, n_seg // ru, body, 0)

    run_segment(ctx_refs, (of_ctx, ob_ctx), n_ctx_chunks)
    run_segment(lat_refs, (of_lat, ob_lat), n_lat_chunks)
    readout((of_lat, ob_lat), lat_refs[6], out_refs[0], n_lat_chunks)
    if ctx_out:
        readout((of_ctx, ob_ctx), ctx_refs[6], out_refs[1], n_ctx_chunks)


def _scan(lat, ctx, gain, *, hpg, ctx_out, name):
    B, T, HK = lat[0].shape
    n_ctx = ctx[0].shape[1]
    n_groups = HK // LANES
    wv = hpg * LANES
    unroll = SCAN_UNROLL
    for rows in (T, n_ctx):
        assert rows % CHUNK == 0 and (rows // CHUNK) % min(unroll, rows // CHUNK) == 0
    assert (T // CHUNK) % READOUT_UNROLL == 0
    tri, pm = (np.stack(c) for c in zip(_scan_constants(False), _scan_constants(True)))
    tri = jnp.asarray(tri, BF16)
    pm = jnp.asarray(pm, F32)

    def seq(rows, widths):
        return [pl.BlockSpec((1, rows, w_), lambda b, j: (b, 0, j)) for w_ in widths]

    widths = (LANES,) * 5 + (wv, wv)
    out_rows = (T, n_ctx) if ctx_out else (T,)
    kern = functools.partial(_scan_kernel, hpg=hpg, n_lat_chunks=T // CHUNK, n_ctx_chunks=n_ctx // CHUNK,
                             ctx_out=ctx_out, unroll=unroll)
    return pl.pallas_call(
        kern,
        grid=(B, n_groups),
        in_specs=seq(T, widths) + seq(n_ctx, widths)
        + [_const_spec(gain.shape), _const_spec(tri.shape), _const_spec(pm.shape)],
        out_specs=[pl.BlockSpec((1, r, wv), lambda b, j: (b, 0, j)) for r in out_rows],
        out_shape=[jax.ShapeDtypeStruct((B, r, n_groups * wv), BF16) for r in out_rows],
        scratch_shapes=[pltpu.VMEM((T, wv), F32), pltpu.VMEM((T, wv), F32),
                        pltpu.VMEM((n_ctx, wv), F32), pltpu.VMEM((n_ctx, wv), F32),
                        pltpu.VMEM((2, wv, LANES), F32)],
        compiler_params=_params("arbitrary", "arbitrary"),
        name=name,
    )(*lat, *ctx, gain, tri, pm)


def _attn_kernel(q_ref, *refs):
    o_ref = refs[-1]
    kv_refs = [(refs[i], refs[i + 1]) for i in range(0, len(refs) - 1, 2)]
    q2 = q_ref[0]
    lane = lax.broadcasted_iota(jnp.int32, q2.shape, 1)
    outs = []
    for kv in range(ATT_KV_HEADS):
        qm = jnp.where(lane // ATT_HEAD_DIM == kv, q2, jnp.zeros_like(q2))
        ss = [_dot_nt(qm, k_ref[0]) for k_ref, _ in kv_refs]
        m = functools.reduce(jnp.maximum, [jnp.max(s, axis=-1, keepdims=True) for s in ss])
        pv = None
        for s, (_, v_ref) in zip(ss, kv_refs):
            p = jnp.exp2((s - m).astype(BF16))
            term = _dot(p, v_ref[0, :, kv * LANES:(kv + 1) * LANES])
            pv = term if pv is None else pv + term
        outs.append(pv / pltpu.roll(pv, ATT_HEAD_DIM, 1))
    o_ref[0] = jnp.where(lane // ATT_HEAD_DIM == 0, outs[0], outs[1]).astype(o_ref.dtype)


def _attention(aq, kvs, name):
    B, R, W = aq.shape
    tq = min(ATT_Q_TILE, R)
    assert R % tq == 0
    in_specs = [pl.BlockSpec((1, tq, LANES), lambda b, i, j: (b, i, j))]
    args = [aq]
    for k, v in kvs:
        in_specs += [pl.BlockSpec((1,) + k.shape[1:], lambda b, i, j: (b, 0, 0)),
                     pl.BlockSpec((1,) + v.shape[1:], lambda b, i, j: (b, 0, 0))]
        args += [k, v]
    return pl.pallas_call(
        _attn_kernel,
        grid=(B, R // tq, W // LANES),
        in_specs=in_specs,
        out_specs=pl.BlockSpec((1, tq, LANES), lambda b, i, j: (b, i, j)),
        out_shape=jax.ShapeDtypeStruct((B, R, W), BF16),
        compiler_params=_params("arbitrary", "arbitrary", "arbitrary"),
        name=name,
    )(*args)


FFN_COL_TILE = 256


def _post_kernel(*refs, n_mix):
    x_ref, mod_ref = refs[0], refs[1]
    mix_refs = refs[2:2 + n_mix]
    wout_refs = refs[2 + n_mix:2 + 2 * n_mix]
    g_pm_ref, g_pf_ref, g_pff_ref, wg_ref, wu_ref, wd_ref, o_ref = refs[2 + 2 * n_mix:]
    D = x_ref.shape[-1]
    n_ff = wg_ref.shape[1]
    m = mod_ref[0]
    subs = _sub_tiles(x_ref.shape[1])
    ys = []
    for sl in subs:
        y = _dot(mix_refs[0][0, sl, :], wout_refs[0][...])
        for mr, wr in zip(mix_refs[1:], wout_refs[1:]):
            y = y + _dot(mr[0, sl, :], wr[...])
        ys.append(y)
    x1s = [x_ref[0, sl, :] + m[:, 2 * D:3 * D] * _rms_rows(y, g_pm_ref[...]) for sl, y in zip(subs, ys)]
    hs = [_modulated_norm(x1, m, g_pf_ref[...], 3, D).astype(BF16) for x1 in x1s]
    acts = []
    for h in hs:
        cols = []
        for c0 in range(0, n_ff, FFN_COL_TILE):
            c1 = min(c0 + FFN_COL_TILE, n_ff)
            cols.append((_silu(_dot(h, wg_ref[:, c0:c1])) * _dot(h, wu_ref[:, c0:c1])).astype(BF16))
        acts.append(jnp.concatenate(cols, axis=1))
    for sl, x1, act in zip(subs, x1s, acts):
        z = _dot(act, wd_ref[...])
        o_ref[0, sl, :] = x1 + m[:, 5 * D:6 * D] * _rms_rows(z, g_pff_ref[...])


def _post(x, mods, mod_row, mixes, consts, tm, name):
    D = x.shape[-1]
    return _row_call(functools.partial(_post_kernel, n_mix=len(mixes)), name, x, mods, mod_row, consts, mixes,
                     (D,), (F32,), tm)[0]


def _deinterleave(n_heads):
    within = np.concatenate([np.arange(0, ATT_HEAD_DIM, 2), np.arange(1, ATT_HEAD_DIM, 2)])
    return (np.arange(n_heads)[:, None] * ATT_HEAD_DIM + within[None, :]).reshape(-1)


def _attn_head_order():
    return np.array([kv * ATT_GROUP + j for j in range(ATT_GROUP) for kv in range(ATT_KV_HEADS)])


def _rope_tables(T):
    rows_n = T // GRID_W
    row = jnp.repeat(jnp.arange(rows_n), GRID_W).astype(F32)
    col = jnp.tile(jnp.arange(GRID_W), rows_n).astype(F32)
    axis_dim = ATT_HEAD_DIM // 2
    inv = ROPE_THETA ** (-jnp.arange(0, axis_dim, 2, dtype=F32) / axis_dim)
    ang = jnp.concatenate([row[:, None] * inv, col[:, None] * inv], axis=-1)
    cos, sin = jnp.cos(ang), jnp.sin(ang)
    reps = LANES // ATT_HEAD_DIM
    cs = jnp.tile(jnp.concatenate([cos, cos], axis=1), (1, reps))
    sn = jnp.tile(jnp.concatenate([-sin, sin], axis=1), (1, reps))
    return cs, sn


def _block_mean_matrix(width, group):
    idx = np.arange(width) // group
    return jnp.asarray((idx[:, None] == idx[None, :]).astype(np.float32) / group, BF16)


def kernel(x, c, ctx, c_ctx, mod_w, mod_b, norm_pre_mix, norm_post_mix, norm_pre_ffn, norm_post_ffn, even_w_in,
           gla_w_gate, gla_b_gate, gla_out_norm, att_q_norm, att_k_norm, even_w_out, odd_w_in, hgrn_lower_bounds,
           hgrn_out_norm, odd_w_out, ffn_w_gate, ffn_w_up, ffn_w_down):
    B, T, D = x.shape
    n_ctx = ctx.shape[1]
    depth = mod_w.shape[0]
    tm_lat = LAT_ROW_TILE
    tm_ctx = min(CTX_ROW_TILE, n_ctx)
    assert T % tm_lat == 0 and n_ctx % tm_ctx == 0 and T % GRID_W == 0 and D % LANES == 0

    n_rows = -(-(B + 1) // 8) * 8
    cond = jnp.concatenate([c, c_ctx[None, :], jnp.zeros((n_rows - B - 1, D), F32)], axis=0)
    mods = _modulation(cond, mod_w, mod_b).reshape(depth * n_rows, 1, 6 * D)

    vec = lambda a: a.reshape(1, -1).astype(F32)
    bf = lambda a: a.astype(BF16)
    rope_tables = _rope_tables(T)
    x_lat, x_ctx = x, ctx

    for l in range(depth):
        last = l == depth - 1
        j = l // 2
        lat_row = lambda b, base=l * n_rows: base + b
        ctx_row = lambda b, base=l * n_rows: base + B
        if l % 2 == 0:
            sizes = np.cumsum([0, GLA_HEADS * GLA_DK, GLA_HEADS * GLA_DK, GLA_HEADS * GLA_DV, GLA_HEADS * GLA_DV,
                               2 * GLA_GATE_RANK, ATT_Q_HEADS * ATT_HEAD_DIM, ATT_KV_HEADS * ATT_HEAD_DIM,
                               ATT_KV_HEADS * ATT_HEAD_DIM])
            o_gq, o_gk, o_gv, o_gog, o_gz, o_aq, o_ak, o_av = sizes[:-1]
            q_cols = o_aq + (_attn_head_order()[:, None] * ATT_HEAD_DIM
                             + _deinterleave(1)[None, :]).reshape(-1)
            k_cols = o_ak + _deinterleave(ATT_KV_HEADS)
            cols = np.concatenate([np.arange(o_gq, o_gz), q_cols, k_cols, np.arange(o_av, sizes[-1]),
                                   np.arange(o_gz, o_aq)])
            w_in = bf(even_w_in[j][:, cols])
            wgate = gla_w_gate[j]
            nk = GLA_HEADS * GLA_DK
            zeros = jnp.zeros((GLA_GATE_RANK, nk), F32)
            wg2 = bf(jnp.concatenate([jnp.concatenate([wgate[0], zeros], axis=1),
                                      jnp.concatenate([zeros, wgate[1]], axis=1)], axis=0))
            bg2 = gla_b_gate[j].reshape(1, 2 * nk).astype(F32)
            qn = vec(jnp.tile(att_q_norm[j][_deinterleave(1)], ATT_Q_HEADS))
            kn = vec(jnp.tile(att_k_norm[j][_deinterleave(1)], ATT_KV_HEADS))
            bdq = _block_mean_matrix(ATT_Q_HEADS * ATT_HEAD_DIM, ATT_HEAD_DIM)
            bdk = _block_mean_matrix(ATT_KV_HEADS * ATT_HEAD_DIM, ATT_HEAD_DIM)
            consts = (vec(norm_pre_mix[l]), w_in, wg2, bg2, qn, kn, bdq, bdk)
            lat = _pre_even(x_lat, mods, lat_row, consts, rope_tables, tm_lat)
            cx = _pre_even(x_ctx, mods, ctx_row, consts, None, tm_ctx)
            pick = lambda o: (o[0], o[1], o[1], o[4], o[5], o[2], o[3])
            mix_a = _scan(pick(lat), pick(cx), vec(gla_out_norm[j]), hpg=LANES // GLA_DK, ctx_out=not last,
                          name="scan_gla")
            k_all = jnp.concatenate([cx[7], lat[7]], axis=1)
            v_all = jnp.concatenate([cx[8], lat[8]], axis=1)
            mix_b_lat = _attention(lat[6], [(k_all, v_all)], "attention_lat")
            mixes_lat = (mix_a[0], mix_b_lat)
            if not last:
                mixes_ctx = (mix_a[1], _attention(cx[6], [(cx[7], cx[8])], "attention_ctx"))
            n_a = GLA_HEADS * GLA_DV
            b_rows = n_a + (_attn_head_order()[:, None] * ATT_HEAD_DIM
                            + np.arange(ATT_HEAD_DIM)[None, :]).reshape(-1)
            wouts = (bf(even_w_out[j][:n_a]), bf(even_w_out[j][b_rows]))
        else:
            w_in = bf(odd_w_in[j])
            lbs = hgrn_lower_bounds.astype(F32)
            lat = _pre_odd(x_lat, mods, lat_row, vec(norm_pre_mix[l]), w_in, lbs, l, tm_lat, "pre_odd_lat")
            cx = _pre_odd(x_ctx, mods, ctx_row, vec(norm_pre_mix[l]), w_in, lbs, l, tm_ctx, "pre_odd_ctx")
            mix = _scan(lat, cx, vec(hgrn_out_norm[j]), hpg=LANES // HGRN_DF, ctx_out=not last, name="scan_hgrn")
            mixes_lat = (mix[0],)
            if not last:
                mixes_ctx = (mix[1],)
            wouts = (bf(odd_w_out[j]),)
        consts = (*wouts, vec(norm_post_mix[l]), vec(norm_pre_ffn[l]), vec(norm_post_ffn[l]),
                  bf(ffn_w_gate[l]), bf(ffn_w_up[l]), bf(ffn_w_down[l]))
        if not last:
            x_ctx = _post(x_ctx, mods, ctx_row, mixes_ctx, consts, tm_ctx, "post_ctx")
        x_lat = _post(x_lat, mods, lat_row, mixes_lat, consts, tm_lat, "post_lat")
    return x_lat
```

```python
import functools

import numpy as np
import jax
import jax.numpy as jnp
from jax import lax
from jax.experimental import pallas as pl
from jax.experimental.pallas import tpu as pltpu

GRID_W = 64
GLA_HEADS = 4
GLA_DK = 64
GLA_DV = 128
GLA_GATE_RANK = 16
GLA_GATE_TAU = 16.0
ATT_Q_HEADS = 8
ATT_KV_HEADS = 2
ATT_HEAD_DIM = 64
ATT_GROUP = ATT_Q_HEADS // ATT_KV_HEADS
ROPE_THETA = 10000.0
HGRN_HEADS = 8
HGRN_DF = 128
EPS = 1e-6

LANES = 128
VMEM_LIMIT_BYTES = 60000 * 1024

LAT_ROW_TILE = 512
CTX_ROW_TILE = 256
SUB_ROWS = 256
ATT_Q_TILE = 256
CHUNK = 128
N_LEVELS = 7
SCAN_UNROLL = 8
READOUT_UNROLL = 4

F32 = jnp.float32
BF16 = jnp.bfloat16
LOG2E = 1.4426950408889634


def _dot(a, b):
    return jnp.dot(a, b, preferred_element_type=F32)


def _dot_nt(a, b):
    return lax.dot_general(a, b, (((1,), (1,)), ((), ())), preferred_element_type=F32)


def _dot_tn(a, b):
    return lax.dot_general(a, b, (((0,), (0,)), ((), ())), preferred_element_type=F32)


def _rms_rows(x, gain):
    return x * lax.rsqrt(jnp.mean(x * x, axis=-1, keepdims=True) + EPS) * gain


def _silu(x):
    return x / (1.0 + jnp.exp(-x))


def _log_sigmoid(x):
    return jnp.minimum(x, 0.0) - jnp.log1p(jnp.exp(-jnp.abs(x)))


def _params(*sem):
    return pltpu.CompilerParams(dimension_semantics=sem, vmem_limit_bytes=VMEM_LIMIT_BYTES)


def _const_spec(shape):
    nd = len(shape)
    return pl.BlockSpec(shape, lambda *_: (0,) * nd, pipeline_mode=pl.Buffered(1))


def _sub_tiles(n_rows):
    sub = min(SUB_ROWS, n_rows)
    return [slice(s, s + sub) for s in range(0, n_rows, sub)]


def _mod_kernel(s_ref, w_ref, b_ref, o_ref):
    s = _silu(s_ref[...]).astype(BF16)
    o_ref[0] = _dot(s, w_ref[0].astype(BF16)) + b_ref[0]


def _modulation(cond, mod_w, mod_b):
    L, D, D6 = mod_w.shape
    R = cond.shape[0]
    tn = 2048
    return pl.pallas_call(
        _mod_kernel,
        grid=(L, D6 // tn),
        in_specs=[
            pl.BlockSpec((R, D), lambda l, j: (0, 0)),
            pl.BlockSpec((1, D, tn), lambda l, j: (l, 0, j)),
            pl.BlockSpec((1, 1, tn), lambda l, j: (l, 0, j)),
        ],
        out_specs=pl.BlockSpec((1, R, tn), lambda l, j: (l, 0, j)),
        out_shape=jax.ShapeDtypeStruct((L, R, D6), F32),
        compiler_params=_params("arbitrary", "arbitrary"),
        name="modulation",
    )(cond, mod_w, mod_b.reshape(L, 1, D6))


def _modulated_norm(x, mod_row, gain, shift_idx, D):
    shift = mod_row[:, shift_idx * D:(shift_idx + 1) * D]
    scale = mod_row[:, (shift_idx + 1) * D:(shift_idx + 2) * D]
    return _rms_rows(x, gain) * (1.0 + scale) + shift


def _row_call(kernel, name, x, mods, mod_row, consts, row_inputs, out_widths, out_dtypes, tm):
    B, R, D = x.shape
    assert R % tm == 0
    row = lambda w_: pl.BlockSpec((1, tm, w_), lambda b, i: (b, i, 0))
    in_specs = [row(D), pl.BlockSpec((1, 1, mods.shape[-1]), lambda b, i: (mod_row(b), 0, 0))]
    in_specs += [pl.BlockSpec((tm, a.shape[-1]), lambda b, i: (i, 0)) if a.ndim == 2 else row(a.shape[-1])
                 for a in row_inputs]
    in_specs += [_const_spec(a.shape) for a in consts]
    return pl.pallas_call(
        kernel,
        grid=(B, R // tm),
        in_specs=in_specs,
        out_specs=[row(w_) for w_ in out_widths],
        out_shape=[jax.ShapeDtypeStruct((B, R, w_), dt) for w_, dt in zip(out_widths, out_dtypes)],
        compiler_params=_params("arbitrary", "arbitrary"),
        name=name,
    )(x, mods, *row_inputs, *consts)


E_GQ, E_GK, E_GV, E_GOG, E_AQ, E_AK, E_AV, E_GZ, E_END = 0, 256, 512, 1024, 1536, 2048, 2176, 2304, 2336


def _swap_halves(x):
    w = x.shape[-1]
    half = ATT_HEAD_DIM // 2
    lane = lax.broadcasted_iota(jnp.int32, x.shape, x.ndim - 1)
    first_half = (lane % ATT_HEAD_DIM) < half
    return jnp.where(first_half, pltpu.roll(x, w - half, x.ndim - 1), pltpu.roll(x, half, x.ndim - 1))


def _head_norm(y, bd_ref, gain_ref):
    ms = _dot((y * y).astype(BF16), bd_ref[...])
    return y * lax.rsqrt(ms + EPS) * gain_ref[...]


def _rope(y, cs, sn):
    reps = y.shape[-1] // LANES
    cs = jnp.concatenate([cs] * reps, axis=1) if reps > 1 else cs
    sn = jnp.concatenate([sn] * reps, axis=1) if reps > 1 else sn
    return y * cs + _swap_halves(y) * sn


def _pre_even_kernel(*refs, rope):
    x_ref, mod_ref = refs[:2]
    refs = refs[2:]
    if rope:
        cs_ref, sn_ref = refs[:2]
        refs = refs[2:]
    gain_ref, w_ref, wg_ref, bg_ref, qn_ref, kn_ref, bdq_ref, bdk_ref = refs[:8]
    gq_ref, gk_ref, gv_ref, gog_ref, gf_ref, gb_ref, aq_ref, ak_ref, av_ref = refs[8:]
    D = x_ref.shape[-1]
    nk = GLA_HEADS * GLA_DK
    subs = _sub_tiles(x_ref.shape[1])
    hs = [_modulated_norm(x_ref[0, sl, :], mod_ref[0], gain_ref[...], 0, D).astype(BF16) for sl in subs]
    for sl, h in zip(subs, hs):
        acc = _dot(h, w_ref[:, E_GQ:E_GV])
        gq_ref[0, sl, :] = (acc[:, :nk] * (GLA_DK ** -0.5)).astype(BF16)
        gk_ref[0, sl, :] = acc[:, nk:].astype(BF16)
        gv_ref[0, sl, :] = _dot(h, w_ref[:, E_GV:E_GOG]).astype(BF16)
        gog_ref[0, sl, :] = _dot(h, w_ref[:, E_GOG:E_AQ]).astype(BF16)
        aq = _head_norm(_dot(h, w_ref[:, E_AQ:E_AK]), bdq_ref, qn_ref)
        acc = _dot(h, w_ref[:, E_AK:E_END])
        ak = _head_norm(acc[:, :E_AV - E_AK], bdk_ref, kn_ref)
        if rope:
            aq = _rope(aq, cs_ref[sl, :], sn_ref[sl, :])
            ak = _rope(ak, cs_ref[sl, :], sn_ref[sl, :])
        aq_ref[0, sl, :] = (aq * (ATT_HEAD_DIM ** -0.5 * LOG2E)).astype(BF16)
        ak_ref[0, sl, :] = ak.astype(BF16)
        av = acc[:, E_AV - E_AK:E_GZ - E_AK]
        first = lax.broadcasted_iota(jnp.int32, av.shape, 1) < ATT_HEAD_DIM
        av_ref[0, sl, :] = jnp.concatenate([jnp.where(first, av, 1.0), jnp.where(first, 1.0, av)],
                                           axis=1).astype(BF16)
        z = acc[:, E_GZ - E_AK:].astype(BF16)
        g = _log_sigmoid(_dot(z, wg_ref[...]) + bg_ref[...]) * (LOG2E / GLA_GATE_TAU)
        gf_ref[0, sl, :] = g[:, :nk]
        gb_ref[0, sl, :] = g[:, nk:]


def _pre_even(x, mods, mod_row, consts, rope_tables, tm):
    widths = (256, 256, 512, 512, 256, 256, 512, 128, 256)
    dtypes = (BF16, BF16, BF16, BF16, F32, F32, BF16, BF16, BF16)
    rope = rope_tables is not None
    return _row_call(functools.partial(_pre_even_kernel, rope=rope), "pre_even_lat" if rope else "pre_even_ctx",
                     x, mods, mod_row, consts, rope_tables if rope else (), widths, dtypes, tm)


def _pre_odd_kernel(x_ref, mod_ref, gain_ref, w_ref, lb_ref,
                    q_ref, kf_ref, kb_ref, gf_ref, gb_ref, v_ref, og_ref, *, layer):
    D = x_ref.shape[-1]
    F = q_ref.shape[-1]
    subs = _sub_tiles(x_ref.shape[1])
    hs = [_modulated_norm(x_ref[0, sl, :], mod_ref[0], gain_ref[...], 0, D).astype(BF16) for sl in subs]
    lbs = []
    for d in range(2):
        raw = lb_ref[d]
        p = jnp.exp(raw - jnp.max(raw, axis=0, keepdims=True))
        p = p / jnp.sum(p, axis=0, keepdims=True)
        lbs.append(jnp.sum(p[1:layer + 1], axis=0, keepdims=True) if layer > 0 else jnp.zeros_like(p[0:1]))
    for sl, h in zip(subs, hs):
        q_ref[0, sl, :] = _silu(_dot(h, w_ref[:, 0:F])).astype(BF16)
        for d, (k_ref, g_ref) in enumerate(((kf_ref, gf_ref), (kb_ref, gb_ref))):
            lb = lbs[d]
            x = _dot(h, w_ref[:, (1 + d) * F:(2 + d) * F])
            t = jnp.exp(-jnp.abs(x))
            r = 1.0 / (1.0 + t)
            tr = t * r
            sig_pos = jnp.where(x >= 0, r, tr)
            sig_neg = jnp.where(x >= 0, tr, r)
            g_ref[0, sl, :] = jnp.log2(lb + (1.0 - lb) * sig_pos)
            k_ref[0, sl, :] = ((1.0 - lb) * sig_neg).astype(BF16)
        v_ref[0, sl, :] = _dot(h, w_ref[:, 3 * F:4 * F]).astype(BF16)
        og_ref[0, sl, :] = _dot(h, w_ref[:, 4 * F:5 * F]).astype(BF16)


def _pre_odd(x, mods, mod_row, gain, w, lower_bounds, layer, tm, name):
    F = w.shape[1] // 5
    dtypes = (BF16, BF16, BF16, F32, F32, BF16, BF16)
    return _row_call(functools.partial(_pre_odd_kernel, layer=layer), name, x, mods, mod_row,
                     (gain, w, lower_bounds), (), (F,) * len(dtypes), dtypes, tm)


def _scan_constants(reverse):
    C = CHUNK
    i = np.arange(C)[:, None]
    t = np.arange(C)[None, :]
    tri = (t >= i) if reverse else (t <= i)
    sg, pm = [], []
    size = C
    while size >= 2:
        half = size // 2
        a = (i // size) * size + half
        a_t = (t // size) * size + half
        same = (i // size) == (t // size)
        qside = (i < a) if reverse else (i >= a)
        sg.append(np.broadcast_to(np.where(qside, 1.0, -1.0), (C, LANES)))
        pm.append(same & qside & ((t >= a_t) if reverse else (t < a_t)))
        size = half
    cat = lambda xs: np.concatenate(xs, 0).astype(np.float32)
    return tri.astype(np.float32), cat(sg), cat(pm)


def _anchor_rows(G, size):
    C = G.shape[0]
    half = size // 2
    if size >= 8:
        parts = [jnp.broadcast_to(G[b * size + half:b * size + half + 1, :], (size, G.shape[1]))
                 for b in range(C // size)]
        return jnp.concatenate(parts, axis=0) if len(parts) > 1 else parts[0]
    row = lax.broadcasted_iota(jnp.int32, G.shape, 0) % size
    out = G
    for r in range(size):
        if r != half:
            out = jnp.where(row == r, pltpu.roll(G, (C - (half - r)) % C, 0), out)
    return out


def _scan_kernel(*refs, hpg, n_lat_chunks, n_ctx_chunks, ctx_out, unroll):
    lat_refs, ctx_refs = refs[0:7], refs[7:14]
    gain_ref, tri_ref, sg_ref, pm_ref = refs[14:18]
    n_out = 2 if ctx_out else 1
    out_refs = refs[18:18 + n_out]
    of_lat, ob_lat, of_ctx, ob_ctx, st_scr = refs[18 + n_out:]
    C = CHUNK
    K = LANES // hpg
    DV = LANES

    st_scr[...] = jnp.zeros_like(st_scr)
    lane = lax.broadcasted_iota(jnp.int32, (C, LANES), 1)
    row = lax.broadcasted_iota(jnp.int32, (C, LANES), 0)
    st_row = lax.broadcasted_iota(jnp.int32, (hpg * DV, LANES), 0)
    st_lane = lax.broadcasted_iota(jnp.int32, (hpg * DV, LANES), 1)
    st_mask = (st_row // DV) == (st_lane // K)

    def load(seg, d, r0):
        q_ref, v_ref = seg[0], seg[5]
        k_ref, g_ref = seg[1 + d], seg[3 + d]
        g = g_ref[0, pl.ds(r0, C), :]
        g_hi = g.astype(BF16)
        g_lo = (g - g_hi.astype(F32)).astype(BF16)
        r = _dot(tri_ref[d], jnp.concatenate([g_hi, g_lo], axis=1))
        return dict(d=d, r0=r0, g=g, cum=r[:, :LANES] + r[:, LANES:],
                    q=q_ref[0, pl.ds(r0, C), :].astype(F32), k=k_ref[0, pl.ds(r0, C), :].astype(F32),
                    v=v_ref[0, pl.ds(r0, C), :])

    def level_scores(it, l):
        d, cum, g = it["d"], it["cum"], it["g"]
        size = C >> l
        half = size // 2
        later_half = (row & half) != 0
        qside = jnp.logical_not(later_half) if d == 1 else later_half
        nxt = lambda n: pltpu.roll(g, C - n, 0)
        if size == 2:
            delta = jnp.where((row & 1) == 0, g if d == 1 else nxt(1), 0.0)
        elif size == 4:
            r4 = row & 3
            if d == 1:
                r0_, r1_, r3_ = g + nxt(1), g, pltpu.roll(g, 1, 0)
            else:
                r0_, r1_, r3_ = nxt(1) + nxt(2), nxt(1), g
            delta = jnp.where(r4 == 0, r0_, jnp.where(r4 == 1, r1_, jnp.where(r4 == 2, 0.0, r3_)))
        else:
            delta = (cum - _anchor_rows(cum, size)) * sg_ref[d, l * C:(l + 1) * C, :]
        t = jnp.where(qside, it["q"], it["k"]) * jnp.exp2(delta)
        pm = pm_ref[d, l * C:(l + 1) * C, :]
        if hpg == 1:
            return _dot_nt(t.astype(BF16), t.astype(BF16)) * pm
        lhs = jnp.concatenate([jnp.where(lane // K == hh, t, 0.0) for hh in range(hpg)], axis=0)
        return _dot_nt(lhs.astype(BF16), t.astype(BF16)) * jnp.concatenate([pm] * hpg, axis=0)

    def finish(it, acc, st, o_scr):
        d, cum, q, k, v = it["d"], it["cum"], it["q"], it["k"], it["v"]
        edge = cum[0:1] if d == 1 else cum[C - 1:C]
        qk = q * k
        inter = _dot_nt((q * jnp.exp2(cum)).astype(BF16), st.astype(BF16))
        outs = []
        for hh in range(hpg):
            v_h = v[:, hh * DV:(hh + 1) * DV]
            diag = jnp.sum(qk if hpg == 1 else jnp.where(lane // K == hh, qk, 0.0), axis=1, keepdims=True)
            a_h = acc[hh * C:(hh + 1) * C].astype(BF16)
            outs.append(_dot(a_h, v_h) + diag * v_h.astype(F32) + inter[:, hh * DV:(hh + 1) * DV])
        o_scr[pl.ds(it["r0"], C), :] = jnp.concatenate(outs, axis=1) if hpg > 1 else outs[0]
        ks = (k * jnp.exp2(edge - cum)).astype(BF16)
        upd = _dot_tn(v, ks)
        if hpg > 1:
            upd = jnp.where(st_mask, upd, 0.0)
        return st * jnp.exp2(edge) + upd

    def trip(seg, o_scrs, n_seg, t, n_u):
        items = []
        for u in range(n_u):
            cf = t * n_u + u
            items.append(load(seg, 0, cf * C if isinstance(cf, int) else pl.multiple_of(cf * C, C)))
            cb = n_seg - 1 - cf
            items.append(load(seg, 1, cb * C if isinstance(cb, int) else pl.multiple_of(cb * C, C)))
        sts = [st_scr[0], st_scr[1]]
        for u in range(n_u):
            pair = items[2 * u:2 * u + 2]
            accs = [level_scores(it, 0) for it in pair]
            for l in range(1, N_LEVELS):
                accs = [acc + level_scores(it, l) for it, acc in zip(pair, accs)]
            for it, acc in zip(pair, accs):
                sts[it["d"]] = finish(it, acc, sts[it["d"]], o_scrs[it["d"]])
        st_scr[0] = sts[0]
        st_scr[1] = sts[1]

    def run_segment(seg, o_scrs, n_seg):
        n_u = min(unroll, n_seg)
        n_trips = n_seg // n_u
        if n_trips == 1:
            trip(seg, o_scrs, n_seg, 0, n_u)
        else:
            def body(t, carry):
                trip(seg, o_scrs, n_seg, t, n_u)
                return carry
            lax.fori_loop(0, n_trips, body, 0)

    def readout_chunks(o_scrs, og_ref, out_ref, c0, n):
        for u in range(n):
            c = c0 + u
            r0 = c * C if isinstance(c, int) else pl.multiple_of(c * C, C)
            o = o_scrs[0][pl.ds(r0, C), :] + o_scrs[1][pl.ds(r0, C), :]
            gate = _silu(og_ref[0, pl.ds(r0, C), :].astype(F32))
            outs = [_rms_rows(o[:, hh * DV:(hh + 1) * DV], gain_ref[...]) * gate[:, hh * DV:(hh + 1) * DV]
                    for hh in range(hpg)]
            y = jnp.concatenate(outs, axis=1) if hpg > 1 else outs[0]
            out_ref[0, pl.ds(r0, C), :] = y.astype(out_ref.dtype)

    def readout(o_scrs, og_ref, out_ref, n_seg):
        ru = min(READOUT_UNROLL, n_seg)
        if n_seg == ru:
            readout_chunks(o_scrs, og_ref, out_ref, 0, ru)
        else:
            def body(t, carry):
                readout_chunks(o_scrs, og_ref, out_ref, t * ru, ru)
                return carry
            lax.fori_loop(0, n_seg // ru, body, 0)

    run_segment(ctx_refs, (of_ctx, ob_ctx), n_ctx_chunks)
    run_segment(lat_refs, (of_lat, ob_lat), n_lat_chunks)
    readout((of_lat, ob_lat), lat_refs[6], out_refs[0], n_lat_chunks)
    if ctx_out:
        readout((of_ctx, ob_ctx), ctx_refs[6], out_refs[1], n_ctx_chunks)


def _scan(lat, ctx, gain, *, hpg, ctx_out, name):
    B, T, HK = lat[0].shape
    n_ctx = ctx[0].shape[1]
    n_groups = HK // LANES
    wv = hpg * LANES
    unroll = SCAN_UNROLL
    for rows in (T, n_ctx):
        assert rows % CHUNK == 0 and (rows // CHUNK) % min(unroll, rows // CHUNK) == 0
    assert (T // CHUNK) % READOUT_UNROLL == 0
    tri, sg, pm = (np.stack(c) for c in zip(_scan_constants(False), _scan_constants(True)))
    tri = jnp.asarray(tri, BF16)
    sg = jnp.asarray(sg, F32)
    pm = jnp.asarray(pm, F32)

    def seq(rows, widths):
        return [pl.BlockSpec((1, rows, w_), lambda b, j: (b, 0, j)) for w_ in widths]

    widths = (LANES,) * 5 + (wv, wv)
    out_rows = (T, n_ctx) if ctx_out else (T,)
    kern = functools.partial(_scan_kernel, hpg=hpg, n_lat_chunks=T // CHUNK, n_ctx_chunks=n_ctx // CHUNK,
                             ctx_out=ctx_out, unroll=unroll)
    return pl.pallas_call(
        kern,
        grid=(B, n_groups),
        in_specs=seq(T, widths) + seq(n_ctx, widths)
        + [_const_spec(a.shape) for a in (gain, tri, sg, pm)],
        out_specs=[pl.BlockSpec((1, r, wv), lambda b, j: (b, 0, j)) for r in out_rows],
        out_shape=[jax.ShapeDtypeStruct((B, r, n_groups * wv), BF16) for r in out_rows],
        scratch_shapes=[pltpu.VMEM((T, wv), F32), pltpu.VMEM((T, wv), F32),
                        pltpu.VMEM((n_ctx, wv), F32), pltpu.VMEM((n_ctx, wv), F32),
                        pltpu.VMEM((2, wv, LANES), F32)],
        compiler_params=_params("arbitrary", "arbitrary"),
        name=name,
    )(*lat, *ctx, gain, tri, sg, pm)


def _attn_kernel(q_ref, *refs):
    o_ref = refs[-1]
    kv_refs = [(refs[i], refs[i + 1]) for i in range(0, len(refs) - 1, 2)]
    q2 = q_ref[0]
    lane = lax.broadcasted_iota(jnp.int32, q2.shape, 1)
    outs = []
    for kv in range(ATT_KV_HEADS):
        qm = jnp.where(lane // ATT_HEAD_DIM == kv, q2, jnp.zeros_like(q2))
        ss = [_dot_nt(qm, k_ref[0]) for k_ref, _ in kv_refs]
        m = functools.reduce(jnp.maximum, [jnp.max(s, axis=-1, keepdims=True) for s in ss])
        pv = None
        for s, (_, v_ref) in zip(ss, kv_refs):
            p = jnp.exp2((s - m).astype(BF16))
            term = _dot(p, v_ref[0, :, kv * LANES:(kv + 1) * LANES])
            pv = term if pv is None else pv + term
        outs.append(pv / pltpu.roll(pv, ATT_HEAD_DIM, 1))
    o_ref[0] = jnp.where(lane // ATT_HEAD_DIM == 0, outs[0], outs[1]).astype(o_ref.dtype)


def _attention(aq, kvs, name):
    B, R, W = aq.shape
    tq = min(ATT_Q_TILE, R)
    assert R % tq == 0
    in_specs = [pl.BlockSpec((1, tq, LANES), lambda b, i, j: (b, i, j))]
    args = [aq]
    for k, v in kvs:
        in_specs += [pl.BlockSpec((1,) + k.shape[1:], lambda b, i, j: (b, 0, 0)),
                     pl.BlockSpec((1,) + v.shape[1:], lambda b, i, j: (b, 0, 0))]
        args += [k, v]
    return pl.pallas_call(
        _attn_kernel,
        grid=(B, R // tq, W // LANES),
        in_specs=in_specs,
        out_specs=pl.BlockSpec((1, tq, LANES), lambda b, i, j: (b, i, j)),
        out_shape=jax.ShapeDtypeStruct((B, R, W), BF16),
        compiler_params=_params("arbitrary", "arbitrary", "arbitrary"),
        name=name,
    )(*args)


FFN_COL_TILE = 256


def _post_kernel(*refs, n_mix):
    x_ref, mod_ref = refs[0], refs[1]
    mix_refs = refs[2:2 + n_mix]
    wout_refs = refs[2 + n_mix:2 + 2 * n_mix]
    g_pm_ref, g_pf_ref, g_pff_ref, wg_ref, wu_ref, wd_ref, o_ref = refs[2 + 2 * n_mix:]
    D = x_ref.shape[-1]
    n_ff = wg_ref.shape[1]
    m = mod_ref[0]
    subs = _sub_tiles(x_ref.shape[1])
    ys = []
    for sl in subs:
        y = _dot(mix_refs[0][0, sl, :], wout_refs[0][...])
        for mr, wr in zip(mix_refs[1:], wout_refs[1:]):
            y = y + _dot(mr[0, sl, :], wr[...])
        ys.append(y)
    x1s = [x_ref[0, sl, :] + m[:, 2 * D:3 * D] * _rms_rows(y, g_pm_ref[...]) for sl, y in zip(subs, ys)]
    hs = [_modulated_norm(x1, m, g_pf_ref[...], 3, D).astype(BF16) for x1 in x1s]
    acts = []
    for h in hs:
        cols = []
        for c0 in range(0, n_ff, FFN_COL_TILE):
            c1 = min(c0 + FFN_COL_TILE, n_ff)
            cols.append((_silu(_dot(h, wg_ref[:, c0:c1])) * _dot(h, wu_ref[:, c0:c1])).astype(BF16))
        acts.append(jnp.concatenate(cols, axis=1))
    for sl, x1, act in zip(subs, x1s, acts):
        z = _dot(act, wd_ref[...])
        o_ref[0, sl, :] = x1 + m[:, 5 * D:6 * D] * _rms_rows(z, g_pff_ref[...])


def _post(x, mods, mod_row, mixes, consts, tm, name):
    D = x.shape[-1]
    return _row_call(functools.partial(_post_kernel, n_mix=len(mixes)), name, x, mods, mod_row, consts, mixes,
                     (D,), (F32,), tm)[0]


def _deinterleave(n_heads):
    within = np.concatenate([np.arange(0, ATT_HEAD_DIM, 2), np.arange(1, ATT_HEAD_DIM, 2)])
    return (np.arange(n_heads)[:, None] * ATT_HEAD_DIM + within[None, :]).reshape(-1)


def _attn_head_order():
    return np.array([kv * ATT_GROUP + j for j in range(ATT_GROUP) for kv in range(ATT_KV_HEADS)])


def _rope_tables(T):
    rows_n = T // GRID_W
    row = jnp.repeat(jnp.arange(rows_n), GRID_W).astype(F32)
    col = jnp.tile(jnp.arange(GRID_W), rows_n).astype(F32)
    axis_dim = ATT_HEAD_DIM // 2
    inv = ROPE_THETA ** (-jnp.arange(0, axis_dim, 2, dtype=F32) / axis_dim)
    ang = jnp.concatenate([row[:, None] * inv, col[:, None] * inv], axis=-1)
    cos, sin = jnp.cos(ang), jnp.sin(ang)
    reps = LANES // ATT_HEAD_DIM
    cs = jnp.tile(jnp.concatenate([cos, cos], axis=1), (1, reps))
    sn = jnp.tile(jnp.concatenate([-sin, sin], axis=1), (1, reps))
    return cs, sn


def _block_mean_matrix(width, group):
    idx = np.arange(width) // group
    return jnp.asarray((idx[:, None] == idx[None, :]).astype(np.float32) / group, BF16)


def kernel(x, c, ctx, c_ctx, mod_w, mod_b, norm_pre_mix, norm_post_mix, norm_pre_ffn, norm_post_ffn, even_w_in,
           gla_w_gate, gla_b_gate, gla_out_norm, att_q_norm, att_k_norm, even_w_out, odd_w_in, hgrn_lower_bounds,
           hgrn_out_norm, odd_w_out, ffn_w_gate, ffn_w_up, ffn_w_down):
    B, T, D = x.shape
    n_ctx = ctx.shape[1]
    depth = mod_w.shape[0]
    tm_lat = LAT_ROW_TILE
    tm_ctx = min(CTX_ROW_TILE, n_ctx)
    assert T % tm_lat == 0 and n_ctx % tm_ctx == 0 and T % GRID_W == 0 and D % LANES == 0

    n_rows = -(-(B + 1) // 8) * 8
    cond = jnp.concatenate([c, c_ctx[None, :], jnp.zeros((n_rows - B - 1, D), F32)], axis=0)
    mods = _modulation(cond, mod_w, mod_b).reshape(depth * n_rows, 1, 6 * D)

    vec = lambda a: a.reshape(1, -1).astype(F32)
    bf = lambda a: a.astype(BF16)
    rope_tables = _rope_tables(T)
    x_lat, x_ctx = x, ctx

    for l in range(depth):
        last = l == depth - 1
        j = l // 2
        lat_row = lambda b, base=l * n_rows: base + b
        ctx_row = lambda b, base=l * n_rows: base + B
        if l % 2 == 0:
            sizes = np.cumsum([0, GLA_HEADS * GLA_DK, GLA_HEADS * GLA_DK, GLA_HEADS * GLA_DV, GLA_HEADS * GLA_DV,
                               2 * GLA_GATE_RANK, ATT_Q_HEADS * ATT_HEAD_DIM, ATT_KV_HEADS * ATT_HEAD_DIM,
                               ATT_KV_HEADS * ATT_HEAD_DIM])
            o_gq, o_gk, o_gv, o_gog, o_gz, o_aq, o_ak, o_av = sizes[:-1]
            q_cols = o_aq + (_attn_head_order()[:, None] * ATT_HEAD_DIM
                             + _deinterleave(1)[None, :]).reshape(-1)
            k_cols = o_ak + _deinterleave(ATT_KV_HEADS)
            cols = np.concatenate([np.arange(o_gq, o_gz), q_cols, k_cols, np.arange(o_av, sizes[-1]),
                                   np.arange(o_gz, o_aq)])
            w_in = bf(even_w_in[j][:, cols])
            wgate = gla_w_gate[j]
            nk = GLA_HEADS * GLA_DK
            zeros = jnp.zeros((GLA_GATE_RANK, nk), F32)
            wg2 = bf(jnp.concatenate([jnp.concatenate([wgate[0], zeros], axis=1),
                                      jnp.concatenate([zeros, wgate[1]], axis=1)], axis=0))
            bg2 = gla_b_gate[j].reshape(1, 2 * nk).astype(F32)
            qn = vec(jnp.tile(att_q_norm[j][_deinterleave(1)], ATT_Q_HEADS))
            kn = vec(jnp.tile(att_k_norm[j][_deinterleave(1)], ATT_KV_HEADS))
            bdq = _block_mean_matrix(ATT_Q_HEADS * ATT_HEAD_DIM, ATT_HEAD_DIM)
            bdk = _block_mean_matrix(ATT_KV_HEADS * ATT_HEAD_DIM, ATT_HEAD_DIM)
            consts = (vec(norm_pre_mix[l]), w_in, wg2, bg2, qn, kn, bdq, bdk)
            lat = _pre_even(x_lat, mods, lat_row, consts, rope_tables, tm_lat)
            cx = _pre_even(x_ctx, mods, ctx_row, consts, None, tm_ctx)
            pick = lambda o: (o[0], o[1], o[1], o[4], o[5], o[2], o[3])
            mix_a = _scan(pick(lat), pick(cx), vec(gla_out_norm[j]), hpg=LANES // GLA_DK, ctx_out=not last,
                          name="scan_gla")
            k_all = jnp.concatenate([cx[7], lat[7]], axis=1)
            v_all = jnp.concatenate([cx[8], lat[8]], axis=1)
            mix_b_lat = _attention(lat[6], [(k_all, v_all)], "attention_lat")
            mixes_lat = (mix_a[0], mix_b_lat)
            if not last:
                mixes_ctx = (mix_a[1], _attention(cx[6], [(cx[7], cx[8])], "attention_ctx"))
            n_a = GLA_HEADS * GLA_DV
            b_rows = n_a + (_attn_head_order()[:, None] * ATT_HEAD_DIM
                            + np.arange(ATT_HEAD_DIM)[None, :]).reshape(-1)
            wouts = (bf(even_w_out[j][:n_a]), bf(even_w_out[j][b_rows]))
        else:
            w_in = bf(odd_w_in[j])
            lbs = hgrn_lower_bounds.astype(F32)
            lat = _pre_odd(x_lat, mods, lat_row, vec(norm_pre_mix[l]), w_in, lbs, l, tm_lat, "pre_odd_lat")
            cx = _pre_odd(x_ctx, mods, ctx_row, vec(norm_pre_mix[l]), w_in, lbs, l, tm_ctx, "pre_odd_ctx")
            mix = _scan(lat, cx, vec(hgrn_out_norm[j]), hpg=LANES // HGRN_DF, ctx_out=not last, name="scan_hgrn")
            mixes_lat = (mix[0],)
            if not last:
                mixes_ctx = (mix[1],)
            wouts = (bf(odd_w_out[j]),)
        consts = (*wouts, vec(norm_post_mix[l]), vec(norm_pre_ffn[l]), vec(norm_post_ffn[l]),
                  bf(ffn_w_gate[l]), bf(ffn_w_up[l]), bf(ffn_w_down[l]))
        if not last:
            x_ctx = _post(x_ctx, mods, ctx_row, mixes_ctx, consts, tm_ctx, "post_ctx")
        x_lat = _post(x_lat, mods, lat_row, mixes_lat, consts, tm_lat, "post_lat")
    return x_lat
```

```python
import functools

import numpy as np
import jax
import jax.numpy as jnp
from jax import lax
from jax.experimental import pallas as pl
from jax.experimental.pallas import tpu as pltpu

GRID_W = 64
GLA_HEADS = 4
GLA_DK = 64
GLA_DV = 128
GLA_GATE_RANK = 16
GLA_GATE_TAU = 16.0
ATT_Q_HEADS = 8
ATT_KV_HEADS = 2
ATT_HEAD_DIM = 64
ATT_GROUP = ATT_Q_HEADS // ATT_KV_HEADS
ROPE_THETA = 10000.0
HGRN_HEADS = 8
HGRN_DF = 128
EPS = 1e-6

LANES = 128
VMEM_LIMIT_BYTES = 60000 * 1024

LAT_ROW_TILE = 512
CTX_ROW_TILE = 256
SUB_ROWS = 256
ATT_Q_TILE = 256
ATT_Q_GROUPS = 4
CHUNK = 128
N_LEVELS = 7
SCAN_UNROLL = 8
READOUT_UNROLL = 4

F32 = jnp.float32
BF16 = jnp.bfloat16
LOG2E = 1.4426950408889634


def _dot(a, b):
    return jnp.dot(a, b, preferred_element_type=F32)


def _dot_nt(a, b):
    return lax.dot_general(a, b, (((1,), (1,)), ((), ())), preferred_element_type=F32)


def _dot_tn(a, b):
    return lax.dot_general(a, b, (((0,), (0,)), ((), ())), preferred_element_type=F32)


def _rms_rows(x, gain):
    return x * lax.rsqrt(jnp.mean(x * x, axis=-1, keepdims=True) + EPS) * gain


def _silu(x):
    return x / (1.0 + jnp.exp(-x))


def _log_sigmoid(x):
    return jnp.minimum(x, 0.0) - jnp.log1p(jnp.exp(-jnp.abs(x)))


def _params(*sem):
    return pltpu.CompilerParams(dimension_semantics=sem, vmem_limit_bytes=VMEM_LIMIT_BYTES)


def _const_spec(shape):
    nd = len(shape)
    return pl.BlockSpec(shape, lambda *_: (0,) * nd, pipeline_mode=pl.Buffered(1))


def _sub_tiles(n_rows):
    sub = min(SUB_ROWS, n_rows)
    return [slice(s, s + sub) for s in range(0, n_rows, sub)]


def _mod_kernel(s_ref, w_ref, b_ref, o_ref):
    s = _silu(s_ref[...]).astype(BF16)
    o_ref[0] = _dot(s, w_ref[0].astype(BF16)) + b_ref[0]


def _modulation(cond, mod_w, mod_b):
    L, D, D6 = mod_w.shape
    R = cond.shape[0]
    tn = 2048
    return pl.pallas_call(
        _mod_kernel,
        grid=(L, D6 // tn),
        in_specs=[
            pl.BlockSpec((R, D), lambda l, j: (0, 0)),
            pl.BlockSpec((1, D, tn), lambda l, j: (l, 0, j)),
            pl.BlockSpec((1, 1, tn), lambda l, j: (l, 0, j)),
        ],
        out_specs=pl.BlockSpec((1, R, tn), lambda l, j: (l, 0, j)),
        out_shape=jax.ShapeDtypeStruct((L, R, D6), F32),
        compiler_params=_params("arbitrary", "arbitrary"),
        name="modulation",
    )(cond, mod_w, mod_b.reshape(L, 1, D6))


def _modulated_norm(x, mod_row, gain, shift_idx, D):
    shift = mod_row[:, shift_idx * D:(shift_idx + 1) * D]
    scale = mod_row[:, (shift_idx + 1) * D:(shift_idx + 2) * D]
    return _rms_rows(x, gain) * (1.0 + scale) + shift


def _row_call(kernel, name, x, mods, mod_row, consts, row_inputs, out_widths, out_dtypes, tm):
    B, R, D = x.shape
    assert R % tm == 0
    row = lambda w_: pl.BlockSpec((1, tm, w_), lambda b, i: (b, i, 0))
    in_specs = [row(D), pl.BlockSpec((1, 1, mods.shape[-1]), lambda b, i: (mod_row(b), 0, 0))]
    in_specs += [pl.BlockSpec((tm, a.shape[-1]), lambda b, i: (i, 0)) if a.ndim == 2 else row(a.shape[-1])
                 for a in row_inputs]
    in_specs += [_const_spec(a.shape) for a in consts]
    return pl.pallas_call(
        kernel,
        grid=(B, R // tm),
        in_specs=in_specs,
        out_specs=[row(w_) for w_ in out_widths],
        out_shape=[jax.ShapeDtypeStruct((B, R, w_), dt) for w_, dt in zip(out_widths, out_dtypes)],
        compiler_params=_params("arbitrary", "arbitrary"),
        name=name,
    )(x, mods, *row_inputs, *consts)


E_GQ, E_GK, E_GV, E_GOG, E_AQ, E_AK, E_AV, E_GZ, E_END = 0, 256, 512, 1024, 1536, 2048, 2176, 2304, 2336


def _swap_halves(x):
    w = x.shape[-1]
    half = ATT_HEAD_DIM // 2
    lane = lax.broadcasted_iota(jnp.int32, x.shape, x.ndim - 1)
    first_half = (lane % ATT_HEAD_DIM) < half
    return jnp.where(first_half, pltpu.roll(x, w - half, x.ndim - 1), pltpu.roll(x, half, x.ndim - 1))


def _head_norm(y, bd_ref, gain_ref):
    ms = _dot((y * y).astype(BF16), bd_ref[...])
    return y * lax.rsqrt(ms + EPS) * gain_ref[...]


def _rope(y, cs, sn):
    reps = y.shape[-1] // LANES
    cs = jnp.concatenate([cs] * reps, axis=1) if reps > 1 else cs
    sn = jnp.concatenate([sn] * reps, axis=1) if reps > 1 else sn
    return y * cs + _swap_halves(y) * sn


def _pre_even_kernel(*refs, rope):
    x_ref, mod_ref = refs[:2]
    refs = refs[2:]
    if rope:
        cs_ref, sn_ref = refs[:2]
        refs = refs[2:]
    gain_ref, w_ref, wg_ref, bg_ref, qn_ref, kn_ref, bdq_ref, bdk_ref = refs[:8]
    gq_ref, gk_ref, gv_ref, gog_ref, gf_ref, gb_ref, aq_ref, ak_ref, av_ref = refs[8:]
    D = x_ref.shape[-1]
    nk = GLA_HEADS * GLA_DK
    subs = _sub_tiles(x_ref.shape[1])
    hs = [_modulated_norm(x_ref[0, sl, :], mod_ref[0], gain_ref[...], 0, D).astype(BF16) for sl in subs]
    for sl, h in zip(subs, hs):
        acc = _dot(h, w_ref[:, E_GQ:E_GV])
        gq_ref[0, sl, :] = (acc[:, :nk] * (GLA_DK ** -0.5)).astype(BF16)
        gk_ref[0, sl, :] = acc[:, nk:].astype(BF16)
        gv_ref[0, sl, :] = _dot(h, w_ref[:, E_GV:E_GOG]).astype(BF16)
        gog_ref[0, sl, :] = _dot(h, w_ref[:, E_GOG:E_AQ]).astype(BF16)
        aq = _head_norm(_dot(h, w_ref[:, E_AQ:E_AK]), bdq_ref, qn_ref)
        acc = _dot(h, w_ref[:, E_AK:E_END])
        ak = _head_norm(acc[:, :E_AV - E_AK], bdk_ref, kn_ref)
        if rope:
            aq = _rope(aq, cs_ref[sl, :], sn_ref[sl, :])
            ak = _rope(ak, cs_ref[sl, :], sn_ref[sl, :])
        aq_ref[0, sl, :] = (aq * (ATT_HEAD_DIM ** -0.5 * LOG2E)).astype(BF16)
        ak_ref[0, sl, :] = ak.astype(BF16)
        av = acc[:, E_AV - E_AK:E_GZ - E_AK]
        first = lax.broadcasted_iota(jnp.int32, av.shape, 1) < ATT_HEAD_DIM
        av_ref[0, sl, :] = jnp.concatenate([jnp.where(first, av, 1.0), jnp.where(first, 1.0, av)],
                                           axis=1).astype(BF16)
        z = acc[:, E_GZ - E_AK:].astype(BF16)
        g = _log_sigmoid(_dot(z, wg_ref[...]) + bg_ref[...]) * (LOG2E / GLA_GATE_TAU)
        gf_ref[0, sl, :] = g[:, :nk]
        gb_ref[0, sl, :] = g[:, nk:]


def _pre_even(x, mods, mod_row, consts, rope_tables, tm):
    widths = (256, 256, 512, 512, 256, 256, 512, 128, 256)
    dtypes = (BF16, BF16, BF16, BF16, F32, F32, BF16, BF16, BF16)
    rope = rope_tables is not None
    return _row_call(functools.partial(_pre_even_kernel, rope=rope), "pre_even_lat" if rope else "pre_even_ctx",
                     x, mods, mod_row, consts, rope_tables if rope else (), widths, dtypes, tm)


def _pre_odd_kernel(x_ref, mod_ref, gain_ref, w_ref, lb_ref,
                    q_ref, kf_ref, kb_ref, gf_ref, gb_ref, v_ref, og_ref, *, layer):
    D = x_ref.shape[-1]
    F = q_ref.shape[-1]
    subs = _sub_tiles(x_ref.shape[1])
    hs = [_modulated_norm(x_ref[0, sl, :], mod_ref[0], gain_ref[...], 0, D).astype(BF16) for sl in subs]
    lbs = []
    for d in range(2):
        raw = lb_ref[d]
        p = jnp.exp(raw - jnp.max(raw, axis=0, keepdims=True))
        p = p / jnp.sum(p, axis=0, keepdims=True)
        lbs.append(jnp.sum(p[1:layer + 1], axis=0, keepdims=True) if layer > 0 else jnp.zeros_like(p[0:1]))
    for sl, h in zip(subs, hs):
        q_ref[0, sl, :] = _silu(_dot(h, w_ref[:, 0:F])).astype(BF16)
        for d, (k_ref, g_ref) in enumerate(((kf_ref, gf_ref), (kb_ref, gb_ref))):
            lb = lbs[d]
            x = _dot(h, w_ref[:, (1 + d) * F:(2 + d) * F])
            t = jnp.exp(-jnp.abs(x))
            r = 1.0 / (1.0 + t)
            tr = t * r
            sig_pos = jnp.where(x >= 0, r, tr)
            sig_neg = jnp.where(x >= 0, tr, r)
            g_ref[0, sl, :] = jnp.log2(lb + (1.0 - lb) * sig_pos)
            k_ref[0, sl, :] = ((1.0 - lb) * sig_neg).astype(BF16)
        v_ref[0, sl, :] = _dot(h, w_ref[:, 3 * F:4 * F]).astype(BF16)
        og_ref[0, sl, :] = _dot(h, w_ref[:, 4 * F:5 * F]).astype(BF16)


def _pre_odd(x, mods, mod_row, gain, w, lower_bounds, layer, tm, name):
    F = w.shape[1] // 5
    dtypes = (BF16, BF16, BF16, F32, F32, BF16, BF16)
    return _row_call(functools.partial(_pre_odd_kernel, layer=layer), name, x, mods, mod_row,
                     (gain, w, lower_bounds), (), (F,) * len(dtypes), dtypes, tm)


def _scan_constants(reverse):
    C = CHUNK
    i = np.arange(C)[:, None]
    t = np.arange(C)[None, :]
    tri = (t >= i) if reverse else (t <= i)
    sg, pm = [], []
    size = C
    while size >= 2:
        half = size // 2
        a = (i // size) * size + half
        a_t = (t // size) * size + half
        same = (i // size) == (t // size)
        qside = (i < a) if reverse else (i >= a)
        sg.append(np.broadcast_to(np.where(qside, 1.0, -1.0), (C, LANES)))
        pm.append(same & qside & ((t >= a_t) if reverse else (t < a_t)))
        size = half
    cat = lambda xs: np.concatenate(xs, 0).astype(np.float32)
    return tri.astype(np.float32), cat(sg), cat(pm)


def _anchor_rows(G, size):
    C = G.shape[0]
    half = size // 2
    if size >= 8:
        parts = [jnp.broadcast_to(G[b * size + half:b * size + half + 1, :], (size, G.shape[1]))
                 for b in range(C // size)]
        return jnp.concatenate(parts, axis=0) if len(parts) > 1 else parts[0]
    row = lax.broadcasted_iota(jnp.int32, G.shape, 0) % size
    out = G
    for r in range(size):
        if r != half:
            out = jnp.where(row == r, pltpu.roll(G, (C - (half - r)) % C, 0), out)
    return out


def _scan_kernel(*refs, hpg, n_lat_chunks, n_ctx_chunks, ctx_out, unroll):
    lat_refs, ctx_refs = refs[0:7], refs[7:14]
    gain_ref, tri_ref, sg_ref, pm_ref = refs[14:18]
    n_out = 2 if ctx_out else 1
    out_refs = refs[18:18 + n_out]
    of_lat, ob_lat, of_ctx, ob_ctx, st_scr = refs[18 + n_out:]
    C = CHUNK
    K = LANES // hpg
    DV = LANES

    st_scr[...] = jnp.zeros_like(st_scr)
    lane = lax.broadcasted_iota(jnp.int32, (C, LANES), 1)
    row = lax.broadcasted_iota(jnp.int32, (C, LANES), 0)
    st_row = lax.broadcasted_iota(jnp.int32, (hpg * DV, LANES), 0)
    st_lane = lax.broadcasted_iota(jnp.int32, (hpg * DV, LANES), 1)
    st_mask = (st_row // DV) == (st_lane // K)

    def load(seg, d, r0):
        q_ref, v_ref = seg[0], seg[5]
        k_ref, g_ref = seg[1 + d], seg[3 + d]
        g = g_ref[0, pl.ds(r0, C), :]
        g_hi = g.astype(BF16)
        g_lo = (g - g_hi.astype(F32)).astype(BF16)
        r = _dot(tri_ref[d], jnp.concatenate([g_hi, g_lo], axis=1))
        return dict(d=d, r0=r0, g=g, cum=r[:, :LANES] + r[:, LANES:],
                    q=q_ref[0, pl.ds(r0, C), :].astype(F32), k=k_ref[0, pl.ds(r0, C), :].astype(F32),
                    v=v_ref[0, pl.ds(r0, C), :])

    def level_scores(it, l):
        d, cum, g = it["d"], it["cum"], it["g"]
        size = C >> l
        half = size // 2
        later_half = (row & half) != 0
        qside = jnp.logical_not(later_half) if d == 1 else later_half
        nxt = lambda n: pltpu.roll(g, C - n, 0)
        if size == 2:
            delta = jnp.where((row & 1) == 0, g if d == 1 else nxt(1), 0.0)
        elif size == 4:
            r4 = row & 3
            if d == 1:
                r0_, r1_, r3_ = g + nxt(1), g, pltpu.roll(g, 1, 0)
            else:
                r0_, r1_, r3_ = nxt(1) + nxt(2), nxt(1), g
            delta = jnp.where(r4 == 0, r0_, jnp.where(r4 == 1, r1_, jnp.where(r4 == 2, 0.0, r3_)))
        else:
            delta = (cum - _anchor_rows(cum, size)) * sg_ref[d, l * C:(l + 1) * C, :]
        t = jnp.where(qside, it["q"], it["k"]) * jnp.exp2(delta)
        pm = pm_ref[d, l * C:(l + 1) * C, :]
        if hpg == 1:
            return _dot_nt(t.astype(BF16), t.astype(BF16)) * pm
        lhs = jnp.concatenate([jnp.where(lane // K == hh, t, 0.0) for hh in range(hpg)], axis=0)
        return _dot_nt(lhs.astype(BF16), t.astype(BF16)) * jnp.concatenate([pm] * hpg, axis=0)

    def finish(it, acc, st, o_scr):
        d, cum, q, k, v = it["d"], it["cum"], it["q"], it["k"], it["v"]
        edge = cum[0:1] if d == 1 else cum[C - 1:C]
        qk = q * k
        inter = _dot_nt((q * jnp.exp2(cum)).astype(BF16), st.astype(BF16))
        outs = []
        for hh in range(hpg):
            v_h = v[:, hh * DV:(hh + 1) * DV]
            diag = jnp.sum(qk if hpg == 1 else jnp.where(lane // K == hh, qk, 0.0), axis=1, keepdims=True)
            a_h = acc[hh * C:(hh + 1) * C].astype(BF16)
            outs.append(_dot(a_h, v_h) + diag * v_h.astype(F32) + inter[:, hh * DV:(hh + 1) * DV])
        o_scr[pl.ds(it["r0"], C), :] = jnp.concatenate(outs, axis=1) if hpg > 1 else outs[0]
        ks = (k * jnp.exp2(edge - cum)).astype(BF16)
        upd = _dot_tn(v, ks)
        if hpg > 1:
            upd = jnp.where(st_mask, upd, 0.0)
        return st * jnp.exp2(edge) + upd

    def trip(seg, o_scrs, n_seg, t, n_u):
        items = []
        for u in range(n_u):
            cf = t * n_u + u
            items.append(load(seg, 0, cf * C if isinstance(cf, int) else pl.multiple_of(cf * C, C)))
            cb = n_seg - 1 - cf
            items.append(load(seg, 1, cb * C if isinstance(cb, int) else pl.multiple_of(cb * C, C)))
        sts = [st_scr[0], st_scr[1]]
        for u in range(n_u):
            pair = items[2 * u:2 * u + 2]
            accs = [level_scores(it, 0) for it in pair]
            for l in range(1, N_LEVELS):
                accs = [acc + level_scores(it, l) for it, acc in zip(pair, accs)]
            for it, acc in zip(pair, accs):
                sts[it["d"]] = finish(it, acc, sts[it["d"]], o_scrs[it["d"]])
        st_scr[0] = sts[0]
        st_scr[1] = sts[1]

    def run_segment(seg, o_scrs, n_seg):
        n_u = min(unroll, n_seg)
        n_trips = n_seg // n_u
        if n_trips == 1:
            trip(seg, o_scrs, n_seg, 0, n_u)
        else:
            def body(t, carry):
                trip(seg, o_scrs, n_seg, t, n_u)
                return carry
            lax.fori_loop(0, n_trips, body, 0)

    def readout_chunks(o_scrs, og_ref, out_ref, c0, n):
        for u in range(n):
            c = c0 + u
            r0 = c * C if isinstance(c, int) else pl.multiple_of(c * C, C)
            o = o_scrs[0][pl.ds(r0, C), :] + o_scrs[1][pl.ds(r0, C), :]
            gate = _silu(og_ref[0, pl.ds(r0, C), :].astype(F32))
            outs = [_rms_rows(o[:, hh * DV:(hh + 1) * DV], gain_ref[...]) * gate[:, hh * DV:(hh + 1) * DV]
                    for hh in range(hpg)]
            y = jnp.concatenate(outs, axis=1) if hpg > 1 else outs[0]
            out_ref[0, pl.ds(r0, C), :] = y.astype(out_ref.dtype)

    def readout(o_scrs, og_ref, out_ref, n_seg):
        ru = min(READOUT_UNROLL, n_seg)
        if n_seg == ru:
            readout_chunks(o_scrs, og_ref, out_ref, 0, ru)
        else:
            def body(t, carry):
                readout_chunks(o_scrs, og_ref, out_ref, t * ru, ru)
                return carry
            lax.fori_loop(0, n_seg // ru, body, 0)

    run_segment(ctx_refs, (of_ctx, ob_ctx), n_ctx_chunks)
    run_segment(lat_refs, (of_lat, ob_lat), n_lat_chunks)
    readout((of_lat, ob_lat), lat_refs[6], out_refs[0], n_lat_chunks)
    if ctx_out:
        readout((of_ctx, ob_ctx), ctx_refs[6], out_refs[1], n_ctx_chunks)


def _scan(lat, ctx, gain, *, hpg, ctx_out, name):
    B, T, HK = lat[0].shape
    n_ctx = ctx[0].shape[1]
    n_groups = HK // LANES
    wv = hpg * LANES
    unroll = SCAN_UNROLL
    for rows in (T, n_ctx):
        assert rows % CHUNK == 0 and (rows // CHUNK) % min(unroll, rows // CHUNK) == 0
    assert (T // CHUNK) % READOUT_UNROLL == 0
    tri, sg, pm = (np.stack(c) for c in zip(_scan_constants(False), _scan_constants(True)))
    tri = jnp.asarray(tri, BF16)
    sg = jnp.asarray(sg, F32)
    pm = jnp.asarray(pm, F32)

    def seq(rows, widths):
        return [pl.BlockSpec((1, rows, w_), lambda b, j: (b, 0, j)) for w_ in widths]

    widths = (LANES,) * 5 + (wv, wv)
    out_rows = (T, n_ctx) if ctx_out else (T,)
    kern = functools.partial(_scan_kernel, hpg=hpg, n_lat_chunks=T // CHUNK, n_ctx_chunks=n_ctx // CHUNK,
                             ctx_out=ctx_out, unroll=unroll)
    return pl.pallas_call(
        kern,
        grid=(B, n_groups),
        in_specs=seq(T, widths) + seq(n_ctx, widths)
        + [_const_spec(a.shape) for a in (gain, tri, sg, pm)],
        out_specs=[pl.BlockSpec((1, r, wv), lambda b, j: (b, 0, j)) for r in out_rows],
        out_shape=[jax.ShapeDtypeStruct((B, r, n_groups * wv), BF16) for r in out_rows],
        scratch_shapes=[pltpu.VMEM((T, wv), F32), pltpu.VMEM((T, wv), F32),
                        pltpu.VMEM((n_ctx, wv), F32), pltpu.VMEM((n_ctx, wv), F32),
                        pltpu.VMEM((2, wv, LANES), F32)],
        compiler_params=_params("arbitrary", "arbitrary"),
        name=name,
    )(*lat, *ctx, gain, tri, sg, pm)


def _attn_kernel(q_ref, *refs):
    o_ref = refs[-1]
    kv_refs = [(refs[i], refs[i + 1]) for i in range(0, len(refs) - 1, 2)]
    lane = lax.broadcasted_iota(jnp.int32, (q_ref.shape[1], LANES), 1)
    for grp in range(q_ref.shape[2] // LANES):
        cols = slice(grp * LANES, (grp + 1) * LANES)
        q2 = q_ref[0, :, cols]
        outs = []
        for kv in range(ATT_KV_HEADS):
            qm = jnp.where(lane // ATT_HEAD_DIM == kv, q2, jnp.zeros_like(q2))
            ss = [_dot_nt(qm, k_ref[0]) for k_ref, _ in kv_refs]
            m = functools.reduce(jnp.maximum, [jnp.max(s, axis=-1, keepdims=True) for s in ss])
            pv = None
            for s, (_, v_ref) in zip(ss, kv_refs):
                p = jnp.exp2((s - m).astype(BF16))
                term = _dot(p, v_ref[0, :, kv * LANES:(kv + 1) * LANES])
                pv = term if pv is None else pv + term
            outs.append(pv / pltpu.roll(pv, ATT_HEAD_DIM, 1))
        o_ref[0, :, cols] = jnp.where(lane // ATT_HEAD_DIM == 0, outs[0], outs[1]).astype(o_ref.dtype)


def _attention(aq, kvs, name):
    B, R, W = aq.shape
    tq = min(ATT_Q_TILE, R)
    assert R % tq == 0
    wq = ATT_Q_GROUPS * LANES
    in_specs = [pl.BlockSpec((1, tq, wq), lambda b, i, j: (b, i, j))]
    args = [aq]
    for k, v in kvs:
        in_specs += [pl.BlockSpec((1,) + k.shape[1:], lambda b, i, j: (b, 0, 0)),
                     pl.BlockSpec((1,) + v.shape[1:], lambda b, i, j: (b, 0, 0))]
        args += [k, v]
    return pl.pallas_call(
        _attn_kernel,
        grid=(B, R // tq, W // wq),
        in_specs=in_specs,
        out_specs=pl.BlockSpec((1, tq, wq), lambda b, i, j: (b, i, j)),
        out_shape=jax.ShapeDtypeStruct((B, R, W), BF16),
        compiler_params=_params("arbitrary", "arbitrary", "arbitrary"),
        name=name,
    )(*args)


FFN_COL_TILE = 256


def _post_kernel(*refs, n_mix):
    x_ref, mod_ref = refs[0], refs[1]
    mix_refs = refs[2:2 + n_mix]
    wout_refs = refs[2 + n_mix:2 + 2 * n_mix]
    g_pm_ref, g_pf_ref, g_pff_ref, wg_ref, wu_ref, wd_ref, o_ref = refs[2 + 2 * n_mix:]
    D = x_ref.shape[-1]
    n_ff = wg_ref.shape[1]
    m = mod_ref[0]
    subs = _sub_tiles(x_ref.shape[1])
    ys = []
    for sl in subs:
        y = _dot(mix_refs[0][0, sl, :], wout_refs[0][...])
        for mr, wr in zip(mix_refs[1:], wout_refs[1:]):
            y = y + _dot(mr[0, sl, :], wr[...])
        ys.append(y)
    x1s = [x_ref[0, sl, :] + m[:, 2 * D:3 * D] * _rms_rows(y, g_pm_ref[...]) for sl, y in zip(subs, ys)]
    hs = [_modulated_norm(x1, m, g_pf_ref[...], 3, D).astype(BF16) for x1 in x1s]
    acts = []
    for h in hs:
        cols = []
        for c0 in range(0, n_ff, FFN_COL_TILE):
            c1 = min(c0 + FFN_COL_TILE, n_ff)
            cols.append((_silu(_dot(h, wg_ref[:, c0:c1])) * _dot(h, wu_ref[:, c0:c1])).astype(BF16))
        acts.append(jnp.concatenate(cols, axis=1))
    for sl, x1, act in zip(subs, x1s, acts):
        z = _dot(act, wd_ref[...])
        o_ref[0, sl, :] = x1 + m[:, 5 * D:6 * D] * _rms_rows(z, g_pff_ref[...])


def _post(x, mods, mod_row, mixes, consts, tm, name):
    D = x.shape[-1]
    return _row_call(functools.partial(_post_kernel, n_mix=len(mixes)), name, x, mods, mod_row, consts, mixes,
                     (D,), (F32,), tm)[0]


def _deinterleave(n_heads):
    within = np.concatenate([np.arange(0, ATT_HEAD_DIM, 2), np.arange(1, ATT_HEAD_DIM, 2)])
    return (np.arange(n_heads)[:, None] * ATT_HEAD_DIM + within[None, :]).reshape(-1)


def _attn_head_order():
    return np.array([kv * ATT_GROUP + j for j in range(ATT_GROUP) for kv in range(ATT_KV_HEADS)])


def _rope_tables(T):
    rows_n = T // GRID_W
    row = jnp.repeat(jnp.arange(rows_n), GRID_W).astype(F32)
    col = jnp.tile(jnp.arange(GRID_W), rows_n).astype(F32)
    axis_dim = ATT_HEAD_DIM // 2
    inv = ROPE_THETA ** (-jnp.arange(0, axis_dim, 2, dtype=F32) / axis_dim)
    ang = jnp.concatenate([row[:, None] * inv, col[:, None] * inv], axis=-1)
    cos, sin = jnp.cos(ang), jnp.sin(ang)
    reps = LANES // ATT_HEAD_DIM
    cs = jnp.tile(jnp.concatenate([cos, cos], axis=1), (1, reps))
    sn = jnp.tile(jnp.concatenate([-sin, sin], axis=1), (1, reps))
    return cs, sn


def _block_mean_matrix(width, group):
    idx = np.arange(width) // group
    return jnp.asarray((idx[:, None] == idx[None, :]).astype(np.float32) / group, BF16)


def kernel(x, c, ctx, c_ctx, mod_w, mod_b, norm_pre_mix, norm_post_mix, norm_pre_ffn, norm_post_ffn, even_w_in,
           gla_w_gate, gla_b_gate, gla_out_norm, att_q_norm, att_k_norm, even_w_out, odd_w_in, hgrn_lower_bounds,
           hgrn_out_norm, odd_w_out, ffn_w_gate, ffn_w_up, ffn_w_down):
    B, T, D = x.shape
    n_ctx = ctx.shape[1]
    depth = mod_w.shape[0]
    tm_lat = LAT_ROW_TILE
    tm_ctx = min(CTX_ROW_TILE, n_ctx)
    assert T % tm_lat == 0 and n_ctx % tm_ctx == 0 and T % GRID_W == 0 and D % LANES == 0

    n_rows = -(-(B + 1) // 8) * 8
    cond = jnp.concatenate([c, c_ctx[None, :], jnp.zeros((n_rows - B - 1, D), F32)], axis=0)
    mods = _modulation(cond, mod_w, mod_b).reshape(depth * n_rows, 1, 6 * D)

    vec = lambda a: a.reshape(1, -1).astype(F32)
    bf = lambda a: a.astype(BF16)
    rope_tables = _rope_tables(T)
    x_lat, x_ctx = x, ctx

    for l in range(depth):
        last = l == depth - 1
        j = l // 2
        lat_row = lambda b, base=l * n_rows: base + b
        ctx_row = lambda b, base=l * n_rows: base + B
        if l % 2 == 0:
            sizes = np.cumsum([0, GLA_HEADS * GLA_DK, GLA_HEADS * GLA_DK, GLA_HEADS * GLA_DV, GLA_HEADS * GLA_DV,
                               2 * GLA_GATE_RANK, ATT_Q_HEADS * ATT_HEAD_DIM, ATT_KV_HEADS * ATT_HEAD_DIM,
                               ATT_KV_HEADS * ATT_HEAD_DIM])
            o_gq, o_gk, o_gv, o_gog, o_gz, o_aq, o_ak, o_av = sizes[:-1]
            q_cols = o_aq + (_attn_head_order()[:, None] * ATT_HEAD_DIM
                             + _deinterleave(1)[None, :]).reshape(-1)
            k_cols = o_ak + _deinterleave(ATT_KV_HEADS)
            cols = np.concatenate([np.arange(o_gq, o_gz), q_cols, k_cols, np.arange(o_av, sizes[-1]),
                                   np.arange(o_gz, o_aq)])
            w_in = bf(even_w_in[j][:, cols])
            wgate = gla_w_gate[j]
            nk = GLA_HEADS * GLA_DK
            zeros = jnp.zeros((GLA_GATE_RANK, nk), F32)
            wg2 = bf(jnp.concatenate([jnp.concatenate([wgate[0], zeros], axis=1),
                                      jnp.concatenate([zeros, wgate[1]], axis=1)], axis=0))
            bg2 = gla_b_gate[j].reshape(1, 2 * nk).astype(F32)
            qn = vec(jnp.tile(att_q_norm[j][_deinterleave(1)], ATT_Q_HEADS))
            kn = vec(jnp.tile(att_k_norm[j][_deinterleave(1)], ATT_KV_HEADS))
            bdq = _block_mean_matrix(ATT_Q_HEADS * ATT_HEAD_DIM, ATT_HEAD_DIM)
            bdk = _block_mean_matrix(ATT_KV_HEADS * ATT_HEAD_DIM, ATT_HEAD_DIM)
            consts = (vec(norm_pre_mix[l]), w_in, wg2, bg2, qn, kn, bdq, bdk)
            lat = _pre_even(x_lat, mods, lat_row, consts, rope_tables, tm_lat)
            cx = _pre_even(x_ctx, mods, ctx_row, consts, None, tm_ctx)
            pick = lambda o: (o[0], o[1], o[1], o[4], o[5], o[2], o[3])
            mix_a = _scan(pick(lat), pick(cx), vec(gla_out_norm[j]), hpg=LANES // GLA_DK, ctx_out=not last,
                          name="scan_gla")
            k_all = jnp.concatenate([cx[7], lat[7]], axis=1)
            v_all = jnp.concatenate([cx[8], lat[8]], axis=1)
            mix_b_lat = _attention(lat[6], [(k_all, v_all)], "attention_lat")
            mixes_lat = (mix_a[0], mix_b_lat)
            if not last:
                mixes_ctx = (mix_a[1], _attention(cx[6], [(cx[7], cx[8])], "attention_ctx"))
            n_a = GLA_HEADS * GLA_DV
            b_rows = n_a + (_attn_head_order()[:, None] * ATT_HEAD_DIM
                            + np.arange(ATT_HEAD_DIM)[None, :]).reshape(-1)
            wouts = (bf(even_w_out[j][:n_a]), bf(even_w_out[j][b_rows]))
        else:
            w_in = bf(odd_w_in[j])
            lbs = hgrn_lower_bounds.astype(F32)
            lat = _pre_odd(x_lat, mods, lat_row, vec(norm_pre_mix[l]), w_in, lbs, l, tm_lat, "pre_odd_lat")
            cx = _pre_odd(x_ctx, mods, ctx_row, vec(norm_pre_mix[l]), w_in, lbs, l, tm_ctx, "pre_odd_ctx")
            mix = _scan(lat, cx, vec(hgrn_out_norm[j]), hpg=LANES // HGRN_DF, ctx_out=not last, name="scan_hgrn")
            mixes_lat = (mix[0],)
            if not last:
                mixes_ctx = (mix[1],)
            wouts = (bf(odd_w_out[j]),)
        consts = (*wouts, vec(norm_post_mix[l]), vec(norm_pre_ffn[l]), vec(norm_post_ffn[l]),
                  bf(ffn_w_gate[l]), bf(ffn_w_up[l]), bf(ffn_w_down[l]))
        if not last:
            x_ctx = _post(x_ctx, mods, ctx_row, mixes_ctx, consts, tm_ctx, "post_ctx")
        x_lat = _post(x_lat, mods, lat_row, mixes_lat, consts, tm_lat, "post_lat")
    return x_lat
```

```python
import functools

import numpy as np
import jax
import jax.numpy as jnp
from jax import lax
from jax.experimental import pallas as pl
from jax.experimental.pallas import tpu as pltpu

GRID_W = 64
GLA_HEADS = 4
GLA_DK = 64
GLA_DV = 128
GLA_GATE_RANK = 16
GLA_GATE_TAU = 16.0
ATT_Q_HEADS = 8
ATT_KV_HEADS = 2
ATT_HEAD_DIM = 64
ATT_GROUP = ATT_Q_HEADS // ATT_KV_HEADS
ROPE_THETA = 10000.0
HGRN_HEADS = 8
HGRN_DF = 128
EPS = 1e-6

LANES = 128
VMEM_LIMIT_BYTES = 60000 * 1024

LAT_ROW_TILE = 512
CTX_ROW_TILE = 256
SUB_ROWS = 256
ATT_Q_TILE = 256
ATT_Q_GROUPS = 4
CHUNK = 128
N_LEVELS = 7
SCAN_UNROLL = 8
READOUT_UNROLL = 4

F32 = jnp.float32
BF16 = jnp.bfloat16
LOG2E = 1.4426950408889634


def _dot(a, b):
    return jnp.dot(a, b, preferred_element_type=F32)


def _dot_nt(a, b):
    return lax.dot_general(a, b, (((1,), (1,)), ((), ())), preferred_element_type=F32)


def _dot_tn(a, b):
    return lax.dot_general(a, b, (((0,), (0,)), ((), ())), preferred_element_type=F32)


def _rms_rows(x, gain):
    return x * lax.rsqrt(jnp.mean(x * x, axis=-1, keepdims=True) + EPS) * gain


def _silu(x):
    return x / (1.0 + jnp.exp(-x))


def _log_sigmoid(x):
    return jnp.minimum(x, 0.0) - jnp.log1p(jnp.exp(-jnp.abs(x)))


def _params(*sem):
    return pltpu.CompilerParams(dimension_semantics=sem, vmem_limit_bytes=VMEM_LIMIT_BYTES)


def _const_spec(shape):
    nd = len(shape)
    return pl.BlockSpec(shape, lambda *_: (0,) * nd, pipeline_mode=pl.Buffered(1))


def _sub_tiles(n_rows):
    sub = min(SUB_ROWS, n_rows)
    return [slice(s, s + sub) for s in range(0, n_rows, sub)]


def _mod_kernel(s_ref, w_ref, b_ref, o_ref):
    s = _silu(s_ref[...]).astype(BF16)
    o_ref[0] = _dot(s, w_ref[0].astype(BF16)) + b_ref[0]


def _modulation(cond, mod_w, mod_b):
    L, D, D6 = mod_w.shape
    R = cond.shape[0]
    tn = 2048
    return pl.pallas_call(
        _mod_kernel,
        grid=(L, D6 // tn),
        in_specs=[
            pl.BlockSpec((R, D), lambda l, j: (0, 0)),
            pl.BlockSpec((1, D, tn), lambda l, j: (l, 0, j)),
            pl.BlockSpec((1, 1, tn), lambda l, j: (l, 0, j)),
        ],
        out_specs=pl.BlockSpec((1, R, tn), lambda l, j: (l, 0, j)),
        out_shape=jax.ShapeDtypeStruct((L, R, D6), F32),
        compiler_params=_params("arbitrary", "arbitrary"),
        name="modulation",
    )(cond, mod_w, mod_b.reshape(L, 1, D6))


def _modulated_norm(x, mod_row, gain, shift_idx, D):
    shift = mod_row[:, shift_idx * D:(shift_idx + 1) * D]
    scale = mod_row[:, (shift_idx + 1) * D:(shift_idx + 2) * D]
    return _rms_rows(x, gain) * (1.0 + scale) + shift


def _row_call(kernel, name, x, mods, mod_row, consts, row_inputs, out_widths, out_dtypes, tm):
    B, R, D = x.shape
    assert R % tm == 0
    row = lambda w_: pl.BlockSpec((1, tm, w_), lambda b, i: (b, i, 0))
    in_specs = [row(D), pl.BlockSpec((1, 1, mods.shape[-1]), lambda b, i: (mod_row(b), 0, 0))]
    in_specs += [pl.BlockSpec((tm, a.shape[-1]), lambda b, i: (i, 0)) if a.ndim == 2 else row(a.shape[-1])
                 for a in row_inputs]
    in_specs += [_const_spec(a.shape) for a in consts]
    return pl.pallas_call(
        kernel,
        grid=(B, R // tm),
        in_specs=in_specs,
        out_specs=[row(w_) for w_ in out_widths],
        out_shape=[jax.ShapeDtypeStruct((B, R, w_), dt) for w_, dt in zip(out_widths, out_dtypes)],
        compiler_params=_params("arbitrary", "arbitrary"),
        name=name,
    )(x, mods, *row_inputs, *consts)


E_GQ, E_GK, E_GV, E_GOG, E_AQ, E_AK, E_AV, E_GZ, E_END = 0, 256, 512, 1024, 1536, 2048, 2176, 2304, 2336


def _swap_halves(x):
    w = x.shape[-1]
    half = ATT_HEAD_DIM // 2
    lane = lax.broadcasted_iota(jnp.int32, x.shape, x.ndim - 1)
    first_half = (lane % ATT_HEAD_DIM) < half
    return jnp.where(first_half, pltpu.roll(x, w - half, x.ndim - 1), pltpu.roll(x, half, x.ndim - 1))


def _head_norm(y, bd_ref, gain_ref):
    ms = _dot((y * y).astype(BF16), bd_ref[...])
    return y * lax.rsqrt(ms + EPS) * gain_ref[...]


def _rope(y, cs, sn):
    reps = y.shape[-1] // LANES
    cs = jnp.concatenate([cs] * reps, axis=1) if reps > 1 else cs
    sn = jnp.concatenate([sn] * reps, axis=1) if reps > 1 else sn
    return y * cs + _swap_halves(y) * sn


def _pre_even_kernel(*refs, rope):
    x_ref, mod_ref = refs[:2]
    refs = refs[2:]
    if rope:
        cs_ref, sn_ref = refs[:2]
        refs = refs[2:]
    gain_ref, w_ref, wg_ref, bg_ref, qn_ref, kn_ref, bdq_ref, bdk_ref = refs[:8]
    gq_ref, gk_ref, gv_ref, gog_ref, gf_ref, gb_ref, aq_ref, ak_ref, av_ref = refs[8:]
    D = x_ref.shape[-1]
    nk = GLA_HEADS * GLA_DK
    subs = _sub_tiles(x_ref.shape[1])
    hs = [_modulated_norm(x_ref[0, sl, :], mod_ref[0], gain_ref[...], 0, D).astype(BF16) for sl in subs]
    for sl, h in zip(subs, hs):
        acc = _dot(h, w_ref[:, E_GQ:E_GV])
        gq_ref[0, sl, :] = (acc[:, :nk] * (GLA_DK ** -0.5)).astype(BF16)
        gk_ref[0, sl, :] = acc[:, nk:].astype(BF16)
        gv_ref[0, sl, :] = _dot(h, w_ref[:, E_GV:E_GOG]).astype(BF16)
        gog_ref[0, sl, :] = _dot(h, w_ref[:, E_GOG:E_AQ]).astype(BF16)
        aq = _head_norm(_dot(h, w_ref[:, E_AQ:E_AK]), bdq_ref, qn_ref)
        acc = _dot(h, w_ref[:, E_AK:E_END])
        ak = _head_norm(acc[:, :E_AV - E_AK], bdk_ref, kn_ref)
        if rope:
            aq = _rope(aq, cs_ref[sl, :], sn_ref[sl, :])
            ak = _rope(ak, cs_ref[sl, :], sn_ref[sl, :])
        aq_ref[0, sl, :] = (aq * (ATT_HEAD_DIM ** -0.5 * LOG2E)).astype(BF16)
        ak_ref[0, sl, :] = ak.astype(BF16)
        av = acc[:, E_AV - E_AK:E_GZ - E_AK]
        first = lax.broadcasted_iota(jnp.int32, av.shape, 1) < ATT_HEAD_DIM
        av_ref[0, sl, :] = jnp.concatenate([jnp.where(first, av, 1.0), jnp.where(first, 1.0, av)],
                                           axis=1).astype(BF16)
        z = acc[:, E_GZ - E_AK:].astype(BF16)
        g = _log_sigmoid(_dot(z, wg_ref[...]) + bg_ref[...]) * (LOG2E / GLA_GATE_TAU)
        gf_ref[0, sl, :] = g[:, :nk]
        gb_ref[0, sl, :] = g[:, nk:]


def _pre_even(x, mods, mod_row, consts, rope_tables, tm):
    widths = (256, 256, 512, 512, 256, 256, 512, 128, 256)
    dtypes = (BF16, BF16, BF16, BF16, F32, F32, BF16, BF16, BF16)
    rope = rope_tables is not None
    return _row_call(functools.partial(_pre_even_kernel, rope=rope), "pre_even_lat" if rope else "pre_even_ctx",
                     x, mods, mod_row, consts, rope_tables if rope else (), widths, dtypes, tm)


def _pre_odd_kernel(x_ref, mod_ref, gain_ref, w_ref, lb_ref,
                    q_ref, kf_ref, kb_ref, gf_ref, gb_ref, v_ref, og_ref, *, layer):
    D = x_ref.shape[-1]
    F = q_ref.shape[-1]
    subs = _sub_tiles(x_ref.shape[1])
    hs = [_modulated_norm(x_ref[0, sl, :], mod_ref[0], gain_ref[...], 0, D).astype(BF16) for sl in subs]
    lbs = []
    for d in range(2):
        raw = lb_ref[d]
        p = jnp.exp(raw - jnp.max(raw, axis=0, keepdims=True))
        p = p / jnp.sum(p, axis=0, keepdims=True)
        lbs.append(jnp.sum(p[1:layer + 1], axis=0, keepdims=True) if layer > 0 else jnp.zeros_like(p[0:1]))
    for sl, h in zip(subs, hs):
        q_ref[0, sl, :] = _silu(_dot(h, w_ref[:, 0:F])).astype(BF16)
        for d, (k_ref, g_ref) in enumerate(((kf_ref, gf_ref), (kb_ref, gb_ref))):
            lb = lbs[d]
            x = _dot(h, w_ref[:, (1 + d) * F:(2 + d) * F])
            t = jnp.exp(-jnp.abs(x))
            r = 1.0 / (1.0 + t)
            tr = t * r
            sig_pos = jnp.where(x >= 0, r, tr)
            sig_neg = jnp.where(x >= 0, tr, r)
            g_ref[0, sl, :] = jnp.log2(lb + (1.0 - lb) * sig_pos)
            k_ref[0, sl, :] = ((1.0 - lb) * sig_neg).astype(BF16)
        v_ref[0, sl, :] = _dot(h, w_ref[:, 3 * F:4 * F]).astype(BF16)
        og_ref[0, sl, :] = _dot(h, w_ref[:, 4 * F:5 * F]).astype(BF16)


def _pre_odd(x, mods, mod_row, gain, w, lower_bounds, layer, tm, name):
    F = w.shape[1] // 5
    dtypes = (BF16, BF16, BF16, F32, F32, BF16, BF16)
    return _row_call(functools.partial(_pre_odd_kernel, layer=layer), name, x, mods, mod_row,
                     (gain, w, lower_bounds), (), (F,) * len(dtypes), dtypes, tm)


def _scan_constants(reverse):
    C = CHUNK
    i = np.arange(C)[:, None]
    t = np.arange(C)[None, :]
    tri = (t >= i) if reverse else (t <= i)
    sg, pm = [], []
    size = C
    while size >= 2:
        half = size // 2
        a = (i // size) * size + half
        a_t = (t // size) * size + half
        same = (i // size) == (t // size)
        qside = (i < a) if reverse else (i >= a)
        sg.append(np.broadcast_to(np.where(qside, 1.0, -1.0), (C, LANES)))
        pm.append(same & qside & ((t >= a_t) if reverse else (t < a_t)))
        size = half
    cat = lambda xs: np.concatenate(xs, 0).astype(np.float32)
    return tri.astype(np.float32), cat(sg), cat(pm)


def _anchor_rows(G, size):
    C = G.shape[0]
    half = size // 2
    if size >= 8:
        parts = [jnp.broadcast_to(G[b * size + half:b * size + half + 1, :], (size, G.shape[1]))
                 for b in range(C // size)]
        return jnp.concatenate(parts, axis=0) if len(parts) > 1 else parts[0]
    row = lax.broadcasted_iota(jnp.int32, G.shape, 0) % size
    out = G
    for r in range(size):
        if r != half:
            out = jnp.where(row == r, pltpu.roll(G, (C - (half - r)) % C, 0), out)
    return out


def _scan_kernel(*refs, hpg, n_lat_chunks, n_ctx_chunks, ctx_out, unroll):
    lat_refs, ctx_refs = refs[0:7], refs[7:14]
    gain_ref, tri_ref, sg_ref, pm_ref = refs[14:18]
    n_out = 2 if ctx_out else 1
    out_refs = refs[18:18 + n_out]
    of_lat, ob_lat, of_ctx, ob_ctx, st_scr = refs[18 + n_out:]
    C = CHUNK
    K = LANES // hpg
    DV = LANES

    st_scr[...] = jnp.zeros_like(st_scr)
    lane = lax.broadcasted_iota(jnp.int32, (C, LANES), 1)
    row = lax.broadcasted_iota(jnp.int32, (C, LANES), 0)
    st_row = lax.broadcasted_iota(jnp.int32, (hpg * DV, LANES), 0)
    st_lane = lax.broadcasted_iota(jnp.int32, (hpg * DV, LANES), 1)
    st_mask = (st_row // DV) == (st_lane // K)

    def load(seg, d, row_starts):
        q_ref, v_ref = seg[0], seg[5]
        k_ref, g_ref = seg[1 + d], seg[3 + d]
        gs = [g_ref[0, pl.ds(r0, C), :] for r0 in row_starts]
        parts = []
        for g in gs:
            g_hi = g.astype(BF16)
            parts += [g_hi, (g - g_hi.astype(F32)).astype(BF16)]
        r = _dot(tri_ref[d], jnp.concatenate(parts, axis=1))
        return [dict(d=d, r0=r0, g=g, cum=r[:, 2 * i * LANES:(2 * i + 1) * LANES]
                     + r[:, (2 * i + 1) * LANES:(2 * i + 2) * LANES],
                     q=q_ref[0, pl.ds(r0, C), :].astype(F32), k=k_ref[0, pl.ds(r0, C), :].astype(F32),
                     v=v_ref[0, pl.ds(r0, C), :]) for i, (r0, g) in enumerate(zip(row_starts, gs))]

    def level_scores(it, l):
        d, cum, g = it["d"], it["cum"], it["g"]
        size = C >> l
        half = size // 2
        later_half = (row & half) != 0
        qside = jnp.logical_not(later_half) if d == 1 else later_half
        nxt = lambda n: pltpu.roll(g, C - n, 0)
        if size == 2:
            delta = jnp.where((row & 1) == 0, g if d == 1 else nxt(1), 0.0)
        elif size == 4:
            r4 = row & 3
            if d == 1:
                r0_, r1_, r3_ = g + nxt(1), g, pltpu.roll(g, 1, 0)
            else:
                r0_, r1_, r3_ = nxt(1) + nxt(2), nxt(1), g
            delta = jnp.where(r4 == 0, r0_, jnp.where(r4 == 1, r1_, jnp.where(r4 == 2, 0.0, r3_)))
        else:
            delta = (cum - _anchor_rows(cum, size)) * sg_ref[d, l * C:(l + 1) * C, :]
        t = jnp.where(qside, it["q"], it["k"]) * jnp.exp2(delta)
        pm = pm_ref[d, l * C:(l + 1) * C, :]
        if hpg == 1:
            return _dot_nt(t.astype(BF16), t.astype(BF16)) * pm
        lhs = jnp.concatenate([jnp.where(lane // K == hh, t, 0.0) for hh in range(hpg)], axis=0)
        return _dot_nt(lhs.astype(BF16), t.astype(BF16)) * jnp.concatenate([pm] * hpg, axis=0)

    def write_output(it, acc, st, o_scr):
        cum, q, k, v = it["cum"], it["q"], it["k"], it["v"]
        qk = q * k
        inter = _dot_nt((q * jnp.exp2(cum)).astype(BF16), st.astype(BF16))
        outs = []
        for hh in range(hpg):
            v_h = v[:, hh * DV:(hh + 1) * DV]
            diag = jnp.sum(qk if hpg == 1 else jnp.where(lane // K == hh, qk, 0.0), axis=1, keepdims=True)
            a_h = acc[hh * C:(hh + 1) * C].astype(BF16)
            outs.append(_dot(a_h, v_h) + diag * v_h.astype(F32) + inter[:, hh * DV:(hh + 1) * DV])
        o_scr[pl.ds(it["r0"], C), :] = jnp.concatenate(outs, axis=1) if hpg > 1 else outs[0]

    def next_state(it, st):
        d, cum = it["d"], it["cum"]
        edge = cum[0:1] if d == 1 else cum[C - 1:C]
        ks = (it["k"] * jnp.exp2(edge - cum)).astype(BF16)
        upd = _dot_tn(it["v"], ks)
        if hpg > 1:
            upd = jnp.where(st_mask, upd, 0.0)
        return st * jnp.exp2(edge) + upd

    def trip(seg, o_scrs, n_seg, t, n_u):
        start = lambda c: c * C if isinstance(c, int) else pl.multiple_of(c * C, C)
        fwd = load(seg, 0, [start(t * n_u + u) for u in range(n_u)])
        bwd = load(seg, 1, [start(n_seg - 1 - (t * n_u + u)) for u in range(n_u)])
        sts = [st_scr[0], st_scr[1]]
        for u in range(n_u):
            pair = [fwd[u], bwd[u]]
            if o_scrs is not None:
                accs = [level_scores(it, 0) for it in pair]
                for l in range(1, N_LEVELS):
                    accs = [acc + level_scores(it, l) for it, acc in zip(pair, accs)]
            for i, it in enumerate(pair):
                if o_scrs is not None:
                    write_output(it, accs[i], sts[it["d"]], o_scrs[it["d"]])
                sts[it["d"]] = next_state(it, sts[it["d"]])
        st_scr[0] = sts[0]
        st_scr[1] = sts[1]

    def run_segment(seg, o_scrs, n_seg):
        n_u = min(unroll, n_seg)
        n_trips = n_seg // n_u
        if n_trips == 1:
            trip(seg, o_scrs, n_seg, 0, n_u)
        else:
            def body(t, carry):
                trip(seg, o_scrs, n_seg, t, n_u)
                return carry
            lax.fori_loop(0, n_trips, body, 0)

    def readout_chunks(o_scrs, og_ref, out_ref, c0, n):
        for u in range(n):
            c = c0 + u
            r0 = c * C if isinstance(c, int) else pl.multiple_of(c * C, C)
            o = o_scrs[0][pl.ds(r0, C), :] + o_scrs[1][pl.ds(r0, C), :]
            gate = _silu(og_ref[0, pl.ds(r0, C), :].astype(F32))
            outs = [_rms_rows(o[:, hh * DV:(hh + 1) * DV], gain_ref[...]) * gate[:, hh * DV:(hh + 1) * DV]
                    for hh in range(hpg)]
            y = jnp.concatenate(outs, axis=1) if hpg > 1 else outs[0]
            out_ref[0, pl.ds(r0, C), :] = y.astype(out_ref.dtype)

    def readout(o_scrs, og_ref, out_ref, n_seg):
        ru = min(READOUT_UNROLL, n_seg)
        if n_seg == ru:
            readout_chunks(o_scrs, og_ref, out_ref, 0, ru)
        else:
            def body(t, carry):
                readout_chunks(o_scrs, og_ref, out_ref, t * ru, ru)
                return carry
            lax.fori_loop(0, n_seg // ru, body, 0)

    run_segment(ctx_refs, (of_ctx, ob_ctx) if ctx_out else None, n_ctx_chunks)
    run_segment(lat_refs, (of_lat, ob_lat), n_lat_chunks)
    readout((of_lat, ob_lat), lat_refs[6], out_refs[0], n_lat_chunks)
    if ctx_out:
        readout((of_ctx, ob_ctx), ctx_refs[6], out_refs[1], n_ctx_chunks)


def _scan(lat, ctx, gain, *, hpg, ctx_out, name):
    B, T, HK = lat[0].shape
    n_ctx = ctx[0].shape[1]
    n_groups = HK // LANES
    wv = hpg * LANES
    unroll = SCAN_UNROLL
    for rows in (T, n_ctx):
        assert rows % CHUNK == 0 and (rows // CHUNK) % min(unroll, rows // CHUNK) == 0
    assert (T // CHUNK) % READOUT_UNROLL == 0
    tri, sg, pm = (np.stack(c) for c in zip(_scan_constants(False), _scan_constants(True)))
    tri = jnp.asarray(tri, BF16)
    sg = jnp.asarray(sg, F32)
    pm = jnp.asarray(pm, F32)

    def seq(rows, widths):
        return [pl.BlockSpec((1, rows, w_), lambda b, j: (b, 0, j)) for w_ in widths]

    widths = (LANES,) * 5 + (wv, wv)
    out_rows = (T, n_ctx) if ctx_out else (T,)
    kern = functools.partial(_scan_kernel, hpg=hpg, n_lat_chunks=T // CHUNK, n_ctx_chunks=n_ctx // CHUNK,
                             ctx_out=ctx_out, unroll=unroll)
    return pl.pallas_call(
        kern,
        grid=(B, n_groups),
        in_specs=seq(T, widths) + seq(n_ctx, widths)
        + [_const_spec(a.shape) for a in (gain, tri, sg, pm)],
        out_specs=[pl.BlockSpec((1, r, wv), lambda b, j: (b, 0, j)) for r in out_rows],
        out_shape=[jax.ShapeDtypeStruct((B, r, n_groups * wv), BF16) for r in out_rows],
        scratch_shapes=[pltpu.VMEM((T, wv), F32), pltpu.VMEM((T, wv), F32),
                        pltpu.VMEM((n_ctx, wv), F32), pltpu.VMEM((n_ctx, wv), F32),
                        pltpu.VMEM((2, wv, LANES), F32)],
        compiler_params=_params("arbitrary", "arbitrary"),
        name=name,
    )(*lat, *ctx, gain, tri, sg, pm)


def _attn_kernel(q_ref, *refs):
    o_ref = refs[-1]
    kv_refs = [(refs[i], refs[i + 1]) for i in range(0, len(refs) - 1, 2)]
    lane = lax.broadcasted_iota(jnp.int32, (q_ref.shape[1], LANES), 1)
    for grp in range(q_ref.shape[2] // LANES):
        cols = slice(grp * LANES, (grp + 1) * LANES)
        q2 = q_ref[0, :, cols]
        outs = []
        for kv in range(ATT_KV_HEADS):
            qm = jnp.where(lane // ATT_HEAD_DIM == kv, q2, jnp.zeros_like(q2))
            ss = [_dot_nt(qm, k_ref[0]) for k_ref, _ in kv_refs]
            m = functools.reduce(jnp.maximum, [jnp.max(s, axis=-1, keepdims=True) for s in ss])
            pv = None
            for s, (_, v_ref) in zip(ss, kv_refs):
                p = jnp.exp2((s - m).astype(BF16))
                term = _dot(p, v_ref[0, :, kv * LANES:(kv + 1) * LANES])
                pv = term if pv is None else pv + term
            outs.append(pv / pltpu.roll(pv, ATT_HEAD_DIM, 1))
        o_ref[0, :, cols] = jnp.where(lane // ATT_HEAD_DIM == 0, outs[0], outs[1]).astype(o_ref.dtype)


def _attention(aq, kvs, name):
    B, R, W = aq.shape
    tq = min(ATT_Q_TILE, R)
    assert R % tq == 0
    wq = ATT_Q_GROUPS * LANES
    in_specs = [pl.BlockSpec((1, tq, wq), lambda b, i, j: (b, i, j))]
    args = [aq]
    for k, v in kvs:
        in_specs += [pl.BlockSpec((1,) + k.shape[1:], lambda b, i, j: (b, 0, 0)),
                     pl.BlockSpec((1,) + v.shape[1:], lambda b, i, j: (b, 0, 0))]
        args += [k, v]
    return pl.pallas_call(
        _attn_kernel,
        grid=(B, R // tq, W // wq),
        in_specs=in_specs,
        out_specs=pl.BlockSpec((1, tq, wq), lambda b, i, j: (b, i, j)),
        out_shape=jax.ShapeDtypeStruct((B, R, W), BF16),
        compiler_params=_params("arbitrary", "arbitrary", "arbitrary"),
        name=name,
    )(*args)


FFN_COL_TILE = 256


def _post_kernel(*refs, n_mix):
    x_ref, mod_ref = refs[0], refs[1]
    mix_refs = refs[2:2 + n_mix]
    wout_refs = refs[2 + n_mix:2 + 2 * n_mix]
    g_pm_ref, g_pf_ref, g_pff_ref, wg_ref, wu_ref, wd_ref, o_ref = refs[2 + 2 * n_mix:]
    D = x_ref.shape[-1]
    n_ff = wg_ref.shape[1]
    m = mod_ref[0]
    subs = _sub_tiles(x_ref.shape[1])
    ys = []
    for sl in subs:
        y = _dot(mix_refs[0][0, sl, :], wout_refs[0][...])
        for mr, wr in zip(mix_refs[1:], wout_refs[1:]):
            y = y + _dot(mr[0, sl, :], wr[...])
        ys.append(y)
    x1s = [x_ref[0, sl, :] + m[:, 2 * D:3 * D] * _rms_rows(y, g_pm_ref[...]) for sl, y in zip(subs, ys)]
    hs = [_modulated_norm(x1, m, g_pf_ref[...], 3, D).astype(BF16) for x1 in x1s]
    acts = []
    for h in hs:
        cols = []
        for c0 in range(0, n_ff, FFN_COL_TILE):
            c1 = min(c0 + FFN_COL_TILE, n_ff)
            cols.append((_silu(_dot(h, wg_ref[:, c0:c1])) * _dot(h, wu_ref[:, c0:c1])).astype(BF16))
        acts.append(jnp.concatenate(cols, axis=1))
    for sl, x1, act in zip(subs, x1s, acts):
        z = _dot(act, wd_ref[...])
        o_ref[0, sl, :] = x1 + m[:, 5 * D:6 * D] * _rms_rows(z, g_pff_ref[...])


def _post(x, mods, mod_row, mixes, consts, tm, name):
    D = x.shape[-1]
    return _row_call(functools.partial(_post_kernel, n_mix=len(mixes)), name, x, mods, mod_row, consts, mixes,
                     (D,), (F32,), tm)[0]


def _deinterleave(n_heads):
    within = np.concatenate([np.arange(0, ATT_HEAD_DIM, 2), np.arange(1, ATT_HEAD_DIM, 2)])
    return (np.arange(n_heads)[:, None] * ATT_HEAD_DIM + within[None, :]).reshape(-1)


def _attn_head_order():
    return np.array([kv * ATT_GROUP + j for j in range(ATT_GROUP) for kv in range(ATT_KV_HEADS)])


def _rope_tables(T):
    rows_n = T // GRID_W
    row = jnp.repeat(jnp.arange(rows_n), GRID_W).astype(F32)
    col = jnp.tile(jnp.arange(GRID_W), rows_n).astype(F32)
    axis_dim = ATT_HEAD_DIM // 2
    inv = ROPE_THETA ** (-jnp.arange(0, axis_dim, 2, dtype=F32) / axis_dim)
    ang = jnp.concatenate([row[:, None] * inv, col[:, None] * inv], axis=-1)
    cos, sin = jnp.cos(ang), jnp.sin(ang)
    reps = LANES // ATT_HEAD_DIM
    cs = jnp.tile(jnp.concatenate([cos, cos], axis=1), (1, reps))
    sn = jnp.tile(jnp.concatenate([-sin, sin], axis=1), (1, reps))
    return cs, sn


def _block_mean_matrix(width, group):
    idx = np.arange(width) // group
    return jnp.asarray((idx[:, None] == idx[None, :]).astype(np.float32) / group, BF16)


def kernel(x, c, ctx, c_ctx, mod_w, mod_b, norm_pre_mix, norm_post_mix, norm_pre_ffn, norm_post_ffn, even_w_in,
           gla_w_gate, gla_b_gate, gla_out_norm, att_q_norm, att_k_norm, even_w_out, odd_w_in, hgrn_lower_bounds,
           hgrn_out_norm, odd_w_out, ffn_w_gate, ffn_w_up, ffn_w_down):
    B, T, D = x.shape
    n_ctx = ctx.shape[1]
    depth = mod_w.shape[0]
    tm_lat = LAT_ROW_TILE
    tm_ctx = min(CTX_ROW_TILE, n_ctx)
    assert T % tm_lat == 0 and n_ctx % tm_ctx == 0 and T % GRID_W == 0 and D % LANES == 0

    n_rows = -(-(B + 1) // 8) * 8
    cond = jnp.concatenate([c, c_ctx[None, :], jnp.zeros((n_rows - B - 1, D), F32)], axis=0)
    mods = _modulation(cond, mod_w, mod_b).reshape(depth * n_rows, 1, 6 * D)

    vec = lambda a: a.reshape(1, -1).astype(F32)
    bf = lambda a: a.astype(BF16)
    rope_tables = _rope_tables(T)
    x_lat, x_ctx = x, ctx

    for l in range(depth):
        last = l == depth - 1
        j = l // 2
        lat_row = lambda b, base=l * n_rows: base + b
        ctx_row = lambda b, base=l * n_rows: base + B
        if l % 2 == 0:
            sizes = np.cumsum([0, GLA_HEADS * GLA_DK, GLA_HEADS * GLA_DK, GLA_HEADS * GLA_DV, GLA_HEADS * GLA_DV,
                               2 * GLA_GATE_RANK, ATT_Q_HEADS * ATT_HEAD_DIM, ATT_KV_HEADS * ATT_HEAD_DIM,
                               ATT_KV_HEADS * ATT_HEAD_DIM])
            o_gq, o_gk, o_gv, o_gog, o_gz, o_aq, o_ak, o_av = sizes[:-1]
            q_cols = o_aq + (_attn_head_order()[:, None] * ATT_HEAD_DIM
                             + _deinterleave(1)[None, :]).reshape(-1)
            k_cols = o_ak + _deinterleave(ATT_KV_HEADS)
            cols = np.concatenate([np.arange(o_gq, o_gz), q_cols, k_cols, np.arange(o_av, sizes[-1]),
                                   np.arange(o_gz, o_aq)])
            w_in = bf(even_w_in[j][:, cols])
            wgate = gla_w_gate[j]
            nk = GLA_HEADS * GLA_DK
            zeros = jnp.zeros((GLA_GATE_RANK, nk), F32)
            wg2 = bf(jnp.concatenate([jnp.concatenate([wgate[0], zeros], axis=1),
                                      jnp.concatenate([zeros, wgate[1]], axis=1)], axis=0))
            bg2 = gla_b_gate[j].reshape(1, 2 * nk).astype(F32)
            qn = vec(jnp.tile(att_q_norm[j][_deinterleave(1)], ATT_Q_HEADS))
            kn = vec(jnp.tile(att_k_norm[j][_deinterleave(1)], ATT_KV_HEADS))
            bdq = _block_mean_matrix(ATT_Q_HEADS * ATT_HEAD_DIM, ATT_HEAD_DIM)
            bdk = _block_mean_matrix(ATT_KV_HEADS * ATT_HEAD_DIM, ATT_HEAD_DIM)
            consts = (vec(norm_pre_mix[l]), w_in, wg2, bg2, qn, kn, bdq, bdk)
            lat = _pre_even(x_lat, mods, lat_row, consts, rope_tables, tm_lat)
            cx = _pre_even(x_ctx, mods, ctx_row, consts, None, tm_ctx)
            pick = lambda o: (o[0], o[1], o[1], o[4], o[5], o[2], o[3])
            mix_a = _scan(pick(lat), pick(cx), vec(gla_out_norm[j]), hpg=LANES // GLA_DK, ctx_out=not last,
                          name="scan_gla")
            k_all = jnp.concatenate([cx[7], lat[7]], axis=1)
            v_all = jnp.concatenate([cx[8], lat[8]], axis=1)
            mix_b_lat = _attention(lat[6], [(k_all, v_all)], "attention_lat")
            mixes_lat = (mix_a[0], mix_b_lat)
            if not last:
                mixes_ctx = (mix_a[1], _attention(cx[6], [(cx[7], cx[8])], "attention_ctx"))
            n_a = GLA_HEADS * GLA_DV
            b_rows = n_a + (_attn_head_order()[:, None] * ATT_HEAD_DIM
                            + np.arange(ATT_HEAD_DIM)[None, :]).reshape(-1)
            wouts = (bf(even_w_out[j][:n_a]), bf(even_w_out[j][b_rows]))
        else:
            w_in = bf(odd_w_in[j])
            lbs = hgrn_lower_bounds.astype(F32)
            lat = _pre_odd(x_lat, mods, lat_row, vec(norm_pre_mix[l]), w_in, lbs, l, tm_lat, "pre_odd_lat")
            cx = _pre_odd(x_ctx, mods, ctx_row, vec(norm_pre_mix[l]), w_in, lbs, l, tm_ctx, "pre_odd_ctx")
            mix = _scan(lat, cx, vec(hgrn_out_norm[j]), hpg=LANES // HGRN_DF, ctx_out=not last, name="scan_hgrn")
            mixes_lat = (mix[0],)
            if not last:
                mixes_ctx = (mix[1],)
            wouts = (bf(odd_w_out[j]),)
        consts = (*wouts, vec(norm_post_mix[l]), vec(norm_pre_ffn[l]), vec(norm_post_ffn[l]),
                  bf(ffn_w_gate[l]), bf(ffn_w_up[l]), bf(ffn_w_down[l]))
        if not last:
            x_ctx = _post(x_ctx, mods, ctx_row, mixes_ctx, consts, tm_ctx, "post_ctx")
        x_lat = _post(x_lat, mods, lat_row, mixes_lat, consts, tm_lat, "post_lat")
    return x_lat
```

```python
import functools

import numpy as np
import jax
import jax.numpy as jnp
from jax import lax
from jax.experimental import pallas as pl
from jax.experimental.pallas import tpu as pltpu

GRID_W = 64
GLA_HEADS = 4
GLA_DK = 64
GLA_DV = 128
GLA_GATE_RANK = 16
GLA_GATE_TAU = 16.0
ATT_Q_HEADS = 8
ATT_KV_HEADS = 2
ATT_HEAD_DIM = 64
ATT_GROUP = ATT_Q_HEADS // ATT_KV_HEADS
ROPE_THETA = 10000.0
HGRN_HEADS = 8
HGRN_DF = 128
EPS = 1e-6

LANES = 128
VMEM_LIMIT_BYTES = 60000 * 1024

LAT_ROW_TILE = 512
CTX_ROW_TILE = 256
SUB_ROWS = 256
MOD_COL_TILE = 2048
FFN_COL_TILE = 256
ATT_Q_TILE = 256
ATT_Q_GROUPS = 4
CHUNK = 128
N_LEVELS = CHUNK.bit_length() - 1
SCAN_UNROLL = 8
READOUT_UNROLL = 4

F32 = jnp.float32
BF16 = jnp.bfloat16
LOG2E = 1.4426950408889634

_SEG_FIELDS = ("q", "k_fwd", "k_bwd", "g_fwd", "g_bwd", "v", "out_gate")


def _dot(a, b):
    return jnp.dot(a, b, preferred_element_type=F32)


def _dot_nt(a, b):
    return lax.dot_general(a, b, (((1,), (1,)), ((), ())), preferred_element_type=F32)


def _dot_tn(a, b):
    return lax.dot_general(a, b, (((0,), (0,)), ((), ())), preferred_element_type=F32)


def _rms_rows(x, gain):
    return x * lax.rsqrt(jnp.mean(x * x, axis=-1, keepdims=True) + EPS) * gain


def _silu(x):
    return x / (1.0 + jnp.exp(-x))


def _log_sigmoid(x):
    return jnp.minimum(x, 0.0) - jnp.log1p(jnp.exp(-jnp.abs(x)))


def _params(*sem):
    return pltpu.CompilerParams(dimension_semantics=sem, vmem_limit_bytes=VMEM_LIMIT_BYTES)


def _const_spec(shape):
    nd = len(shape)
    return pl.BlockSpec(shape, lambda *_: (0,) * nd, pipeline_mode=pl.Buffered(1))


def _sub_tiles(n_rows):
    sub = min(SUB_ROWS, n_rows)
    return [slice(s, s + sub) for s in range(0, n_rows, sub)]


def _mod_kernel(s_ref, w_ref, b_ref, o_ref):
    s = _silu(s_ref[...]).astype(BF16)
    o_ref[0] = _dot(s, w_ref[0].astype(BF16)) + b_ref[0]


def _modulation(cond, mod_w, mod_b):
    L, D, D6 = mod_w.shape
    R = cond.shape[0]
    tn = MOD_COL_TILE
    assert D6 % tn == 0
    return pl.pallas_call(
        _mod_kernel,
        grid=(L, D6 // tn),
        in_specs=[
            pl.BlockSpec((R, D), lambda l, j: (0, 0)),
            pl.BlockSpec((1, D, tn), lambda l, j: (l, 0, j)),
            pl.BlockSpec((1, 1, tn), lambda l, j: (l, 0, j)),
        ],
        out_specs=pl.BlockSpec((1, R, tn), lambda l, j: (l, 0, j)),
        out_shape=jax.ShapeDtypeStruct((L, R, D6), F32),
        compiler_params=_params("arbitrary", "arbitrary"),
        name="modulation",
    )(cond, mod_w, mod_b.reshape(L, 1, D6))


def _modulated_norm(x, mod_row, gain, shift_idx, D):
    shift = mod_row[:, shift_idx * D:(shift_idx + 1) * D]
    scale = mod_row[:, (shift_idx + 1) * D:(shift_idx + 2) * D]
    return _rms_rows(x, gain) * (1.0 + scale) + shift


def _row_call(kernel, name, x, mods, mod_row, consts, row_inputs, out_widths, out_dtypes, tm):
    B, R, D = x.shape
    assert R % tm == 0
    row = lambda w_: pl.BlockSpec((1, tm, w_), lambda b, i: (b, i, 0))
    in_specs = [row(D), pl.BlockSpec((1, 1, mods.shape[-1]), lambda b, i: (mod_row(b), 0, 0))]
    in_specs += [pl.BlockSpec((tm, a.shape[-1]), lambda b, i: (i, 0)) if a.ndim == 2 else row(a.shape[-1])
                 for a in row_inputs]
    in_specs += [_const_spec(a.shape) for a in consts]
    return pl.pallas_call(
        kernel,
        grid=(B, R // tm),
        in_specs=in_specs,
        out_specs=[row(w_) for w_ in out_widths],
        out_shape=[jax.ShapeDtypeStruct((B, R, w_), dt) for w_, dt in zip(out_widths, out_dtypes)],
        compiler_params=_params("arbitrary", "arbitrary"),
        name=name,
    )(x, mods, *row_inputs, *consts)


_EVEN_WIDTHS = (GLA_HEADS * GLA_DK, GLA_HEADS * GLA_DK, GLA_HEADS * GLA_DV, GLA_HEADS * GLA_DV,
                ATT_Q_HEADS * ATT_HEAD_DIM, ATT_KV_HEADS * ATT_HEAD_DIM, ATT_KV_HEADS * ATT_HEAD_DIM,
                2 * GLA_GATE_RANK)
E_GQ, E_GK, E_GV, E_GOG, E_AQ, E_AK, E_AV, E_GZ, E_END = (int(c) for c in np.cumsum((0,) + _EVEN_WIDTHS))


def _swap_halves(x):
    w = x.shape[-1]
    half = ATT_HEAD_DIM // 2
    lane = lax.broadcasted_iota(jnp.int32, x.shape, x.ndim - 1)
    first_half = (lane % ATT_HEAD_DIM) < half
    return jnp.where(first_half, pltpu.roll(x, w - half, x.ndim - 1), pltpu.roll(x, half, x.ndim - 1))


def _head_norm(y, bd_ref, gain_ref):
    ms = _dot((y * y).astype(BF16), bd_ref[...])
    return y * lax.rsqrt(ms + EPS) * gain_ref[...]


def _rope(y, cs, sn):
    reps = y.shape[-1] // LANES
    cs = jnp.concatenate([cs] * reps, axis=1) if reps > 1 else cs
    sn = jnp.concatenate([sn] * reps, axis=1) if reps > 1 else sn
    return y * cs + _swap_halves(y) * sn


def _pre_even_kernel(*refs, rope):
    x_ref, mod_ref = refs[:2]
    refs = refs[2:]
    if rope:
        cs_ref, sn_ref = refs[:2]
        refs = refs[2:]
    gain_ref, w_ref, wg_ref, bg_ref, qn_ref, kn_ref, bdq_ref, bdk_ref = refs[:8]
    gq_ref, gk_ref, gv_ref, gog_ref, gf_ref, gb_ref, aq_ref, ak_ref, av_ref = refs[8:]
    D = x_ref.shape[-1]
    nk = GLA_HEADS * GLA_DK
    subs = _sub_tiles(x_ref.shape[1])
    hs = [_modulated_norm(x_ref[0, sl, :], mod_ref[0], gain_ref[...], 0, D).astype(BF16) for sl in subs]
    for sl, h in zip(subs, hs):
        acc = _dot(h, w_ref[:, E_GQ:E_GV])
        gq_ref[0, sl, :] = (acc[:, :nk] * (GLA_DK ** -0.5)).astype(BF16)
        gk_ref[0, sl, :] = acc[:, nk:].astype(BF16)
        gv_ref[0, sl, :] = _dot(h, w_ref[:, E_GV:E_GOG]).astype(BF16)
        gog_ref[0, sl, :] = _dot(h, w_ref[:, E_GOG:E_AQ]).astype(BF16)
        aq = _head_norm(_dot(h, w_ref[:, E_AQ:E_AK]), bdq_ref, qn_ref)
        acc = _dot(h, w_ref[:, E_AK:E_END])
        ak = _head_norm(acc[:, :E_AV - E_AK], bdk_ref, kn_ref)
        if rope:
            aq = _rope(aq, cs_ref[sl, :], sn_ref[sl, :])
            ak = _rope(ak, cs_ref[sl, :], sn_ref[sl, :])
        aq_ref[0, sl, :] = (aq * (ATT_HEAD_DIM ** -0.5 * LOG2E)).astype(BF16)
        ak_ref[0, sl, :] = ak.astype(BF16)
        av = acc[:, E_AV - E_AK:E_GZ - E_AK]
        first = lax.broadcasted_iota(jnp.int32, av.shape, 1) < ATT_HEAD_DIM
        av_ref[0, sl, :] = jnp.concatenate([jnp.where(first, av, 1.0), jnp.where(first, 1.0, av)],
                                           axis=1).astype(BF16)
        z = acc[:, E_GZ - E_AK:].astype(BF16)
        g = _log_sigmoid(_dot(z, wg_ref[...]) + bg_ref[...]) * (LOG2E / GLA_GATE_TAU)
        gf_ref[0, sl, :] = g[:, :nk]
        gb_ref[0, sl, :] = g[:, nk:]


def _pre_even(x, mods, mod_row, consts, rope_tables, tm):
    nk = GLA_HEADS * GLA_DK
    widths = (nk, nk, GLA_HEADS * GLA_DV, GLA_HEADS * GLA_DV, nk, nk, ATT_Q_HEADS * ATT_HEAD_DIM,
              ATT_KV_HEADS * ATT_HEAD_DIM, ATT_KV_HEADS * LANES)
    dtypes = (BF16, BF16, BF16, BF16, F32, F32, BF16, BF16, BF16)
    rope = rope_tables is not None
    return _row_call(functools.partial(_pre_even_kernel, rope=rope), "pre_even_lat" if rope else "pre_even_ctx",
                     x, mods, mod_row, consts, rope_tables if rope else (), widths, dtypes, tm)


def _pre_odd_kernel(x_ref, mod_ref, gain_ref, w_ref, lb_ref, *out_refs, layer, with_q):
    if with_q:
        q_ref, kf_ref, kb_ref, gf_ref, gb_ref, v_ref, og_ref = out_refs
    else:
        kf_ref, kb_ref, gf_ref, gb_ref, v_ref = out_refs
    D = x_ref.shape[-1]
    F = v_ref.shape[-1]
    subs = _sub_tiles(x_ref.shape[1])
    hs = [_modulated_norm(x_ref[0, sl, :], mod_ref[0], gain_ref[...], 0, D).astype(BF16) for sl in subs]
    lbs = []
    for d in range(2):
        raw = lb_ref[d]
        p = jnp.exp(raw - jnp.max(raw, axis=0, keepdims=True))
        p = p / jnp.sum(p, axis=0, keepdims=True)
        lbs.append(jnp.sum(p[1:layer + 1], axis=0, keepdims=True) if layer > 0 else jnp.zeros_like(p[0:1]))
    for sl, h in zip(subs, hs):
        if with_q:
            q_ref[0, sl, :] = _silu(_dot(h, w_ref[:, 0:F])).astype(BF16)
        for d, (k_ref, g_ref) in enumerate(((kf_ref, gf_ref), (kb_ref, gb_ref))):
            lb = lbs[d]
            x = _dot(h, w_ref[:, (1 + d) * F:(2 + d) * F])
            t = jnp.exp(-jnp.abs(x))
            r = 1.0 / (1.0 + t)
            tr = t * r
            sig_pos = jnp.where(x >= 0, r, tr)
            sig_neg = jnp.where(x >= 0, tr, r)
            g_ref[0, sl, :] = jnp.log2(lb + (1.0 - lb) * sig_pos)
            k_ref[0, sl, :] = ((1.0 - lb) * sig_neg).astype(BF16)
        v_ref[0, sl, :] = _dot(h, w_ref[:, 3 * F:4 * F]).astype(BF16)
        if with_q:
            og_ref[0, sl, :] = _dot(h, w_ref[:, 4 * F:5 * F]).astype(BF16)


def _pre_odd(x, mods, mod_row, gain, w, lower_bounds, layer, tm, name, with_q=True):
    F = w.shape[1] // 5
    fields = _SEG_FIELDS if with_q else _SEG_FIELDS[1:6]
    dtype = dict(zip(_SEG_FIELDS, (BF16, BF16, BF16, F32, F32, BF16, BF16)))
    outs = _row_call(functools.partial(_pre_odd_kernel, layer=layer, with_q=with_q), name, x, mods, mod_row,
                     (gain, w, lower_bounds), (), (F,) * len(fields), [dtype[f] for f in fields], tm)
    return dict(zip(fields, outs))


def _scan_constants(reverse):
    C = CHUNK
    i = np.arange(C)[:, None]
    t = np.arange(C)[None, :]
    tri = (t >= i) if reverse else (t <= i)
    sg, pm = [], []
    size = C
    while size >= 2:
        half = size // 2
        a = (i // size) * size + half
        a_t = (t // size) * size + half
        same = (i // size) == (t // size)
        qside = (i < a) if reverse else (i >= a)
        sg.append(np.broadcast_to(np.where(qside, 1.0, -1.0), (C, LANES)))
        pm.append(same & qside & ((t >= a_t) if reverse else (t < a_t)))
        size = half
    cat = lambda xs: np.concatenate(xs, 0).astype(np.float32)
    return tri.astype(np.float32), cat(sg), cat(pm)


def _anchor_rows(G, size):
    C = G.shape[0]
    half = size // 2
    if size >= 8:
        parts = [jnp.broadcast_to(G[b * size + half:b * size + half + 1, :], (size, G.shape[1]))
                 for b in range(C // size)]
        return jnp.concatenate(parts, axis=0) if len(parts) > 1 else parts[0]
    row = lax.broadcasted_iota(jnp.int32, G.shape, 0) % size
    out = G
    for r in range(size):
        if r != half:
            out = jnp.where(row == r, pltpu.roll(G, (C - (half - r)) % C, 0), out)
    return out


def _scan_kernel(*refs, hpg, n_lat_chunks, n_ctx_chunks, ctx_out, unroll):
    def segment(seg_refs):
        with_q = len(seg_refs) == len(_SEG_FIELDS)
        rest = seg_refs[1:6] if with_q else seg_refs
        return dict(q=seg_refs[0] if with_q else None, k=rest[0:2], g=rest[2:4], v=rest[4],
                    og=seg_refs[6] if with_q else None)

    n_ctx_refs = len(_SEG_FIELDS) if ctx_out else len(_SEG_FIELDS) - 2
    lat_refs = segment(refs[:len(_SEG_FIELDS)])
    refs = refs[len(_SEG_FIELDS):]
    ctx_refs = segment(refs[:n_ctx_refs])
    gain_ref, tri_ref, sg_ref, pm_ref = refs[n_ctx_refs:n_ctx_refs + 4]
    refs = refs[n_ctx_refs + 4:]
    n_out = 2 if ctx_out else 1
    out_refs = refs[:n_out]
    of_lat, ob_lat, of_ctx, ob_ctx, st_scr = refs[n_out:]
    C = CHUNK
    K = LANES // hpg
    DV = LANES

    st_scr[...] = jnp.zeros_like(st_scr)
    lane = lax.broadcasted_iota(jnp.int32, (C, LANES), 1)
    row = lax.broadcasted_iota(jnp.int32, (C, LANES), 0)
    st_row = lax.broadcasted_iota(jnp.int32, (hpg * DV, LANES), 0)
    st_lane = lax.broadcasted_iota(jnp.int32, (hpg * DV, LANES), 1)
    st_mask = (st_row // DV) == (st_lane // K)

    def load(seg, d, row_starts):
        q_ref, v_ref, k_ref, g_ref = seg["q"], seg["v"], seg["k"][d], seg["g"][d]
        gs = [g_ref[0, pl.ds(r0, C), :] for r0 in row_starts]
        parts = []
        for g in gs:
            g_hi = g.astype(BF16)
            parts += [g_hi, (g - g_hi.astype(F32)).astype(BF16)]
        r = _dot(tri_ref[d], jnp.concatenate(parts, axis=1))
        return [dict(d=d, r0=r0, g=g, cum=r[:, 2 * i * LANES:(2 * i + 1) * LANES]
                     + r[:, (2 * i + 1) * LANES:(2 * i + 2) * LANES],
                     q=None if q_ref is None else q_ref[0, pl.ds(r0, C), :].astype(F32),
                     k=k_ref[0, pl.ds(r0, C), :].astype(F32),
                     v=v_ref[0, pl.ds(r0, C), :]) for i, (r0, g) in enumerate(zip(row_starts, gs))]

    def level_scores(it, l):
        d, cum, g = it["d"], it["cum"], it["g"]
        size = C >> l
        half = size // 2
        later_half = (row & half) != 0
        qside = jnp.logical_not(later_half) if d == 1 else later_half
        nxt = lambda n: pltpu.roll(g, C - n, 0)
        if size == 2:
            delta = jnp.where((row & 1) == 0, g if d == 1 else nxt(1), 0.0)
        elif size == 4:
            r4 = row & 3
            if d == 1:
                r0_, r1_, r3_ = g + nxt(1), g, pltpu.roll(g, 1, 0)
            else:
                r0_, r1_, r3_ = nxt(1) + nxt(2), nxt(1), g
            delta = jnp.where(r4 == 0, r0_, jnp.where(r4 == 1, r1_, jnp.where(r4 == 2, 0.0, r3_)))
        else:
            delta = (cum - _anchor_rows(cum, size)) * sg_ref[d, l * C:(l + 1) * C, :]
        t = jnp.where(qside, it["q"], it["k"]) * jnp.exp2(delta)
        pm = pm_ref[d, l * C:(l + 1) * C, :]
        if hpg == 1:
            return _dot_nt(t.astype(BF16), t.astype(BF16)) * pm
        lhs = jnp.concatenate([jnp.where(lane // K == hh, t, 0.0) for hh in range(hpg)], axis=0)
        return _dot_nt(lhs.astype(BF16), t.astype(BF16)) * jnp.concatenate([pm] * hpg, axis=0)

    def write_output(it, acc, st, o_scr):
        cum, q, k, v = it["cum"], it["q"], it["k"], it["v"]
        qk = q * k
        inter = _dot_nt((q * jnp.exp2(cum)).astype(BF16), st.astype(BF16))
        outs = []
        for hh in range(hpg):
            v_h = v[:, hh * DV:(hh + 1) * DV]
            diag = jnp.sum(qk if hpg == 1 else jnp.where(lane // K == hh, qk, 0.0), axis=1, keepdims=True)
            a_h = acc[hh * C:(hh + 1) * C].astype(BF16)
            outs.append(_dot(a_h, v_h) + diag * v_h.astype(F32) + inter[:, hh * DV:(hh + 1) * DV])
        o_scr[pl.ds(it["r0"], C), :] = jnp.concatenate(outs, axis=1) if hpg > 1 else outs[0]

    def next_state(it, st):
        d, cum = it["d"], it["cum"]
        edge = cum[0:1] if d == 1 else cum[C - 1:C]
        ks = (it["k"] * jnp.exp2(edge - cum)).astype(BF16)
        upd = _dot_tn(it["v"], ks)
        if hpg > 1:
            upd = jnp.where(st_mask, upd, 0.0)
        return st * jnp.exp2(edge) + upd

    def trip(seg, o_scrs, n_seg, t, n_u):
        start = lambda c: c * C if isinstance(c, int) else pl.multiple_of(c * C, C)
        fwd = load(seg, 0, [start(t * n_u + u) for u in range(n_u)])
        bwd = load(seg, 1, [start(n_seg - 1 - (t * n_u + u)) for u in range(n_u)])
        sts = [st_scr[0], st_scr[1]]
        for u in range(n_u):
            pair = [fwd[u], bwd[u]]
            if o_scrs is not None:
                accs = [level_scores(it, 0) for it in pair]
                for l in range(1, N_LEVELS):
                    accs = [acc + level_scores(it, l) for it, acc in zip(pair, accs)]
            for i, it in enumerate(pair):
                if o_scrs is not None:
                    write_output(it, accs[i], sts[it["d"]], o_scrs[it["d"]])
                sts[it["d"]] = next_state(it, sts[it["d"]])
        st_scr[0] = sts[0]
        st_scr[1] = sts[1]

    def run_segment(seg, o_scrs, n_seg):
        n_u = min(unroll, n_seg)
        n_trips = n_seg // n_u
        if n_trips == 1:
            trip(seg, o_scrs, n_seg, 0, n_u)
        else:
            def body(t, carry):
                trip(seg, o_scrs, n_seg, t, n_u)
                return carry
            lax.fori_loop(0, n_trips, body, 0)

    def readout_chunks(o_scrs, og_ref, out_ref, c0, n):
        for u in range(n):
            c = c0 + u
            r0 = c * C if isinstance(c, int) else pl.multiple_of(c * C, C)
            o = o_scrs[0][pl.ds(r0, C), :] + o_scrs[1][pl.ds(r0, C), :]
            gate = _silu(og_ref[0, pl.ds(r0, C), :].astype(F32))
            outs = [_rms_rows(o[:, hh * DV:(hh + 1) * DV], gain_ref[...]) * gate[:, hh * DV:(hh + 1) * DV]
                    for hh in range(hpg)]
            y = jnp.concatenate(outs, axis=1) if hpg > 1 else outs[0]
            out_ref[0, pl.ds(r0, C), :] = y.astype(out_ref.dtype)

    def readout(o_scrs, og_ref, out_ref, n_seg):
        ru = min(READOUT_UNROLL, n_seg)
        if n_seg == ru:
            readout_chunks(o_scrs, og_ref, out_ref, 0, ru)
        else:
            def body(t, carry):
                readout_chunks(o_scrs, og_ref, out_ref, t * ru, ru)
                return carry
            lax.fori_loop(0, n_seg // ru, body, 0)

    run_segment(ctx_refs, (of_ctx, ob_ctx) if ctx_out else None, n_ctx_chunks)
    run_segment(lat_refs, (of_lat, ob_lat), n_lat_chunks)
    readout((of_lat, ob_lat), lat_refs["og"], out_refs[0], n_lat_chunks)
    if ctx_out:
        readout((of_ctx, ob_ctx), ctx_refs["og"], out_refs[1], n_ctx_chunks)


def _scan(lat, ctx, gain, *, hpg, ctx_out, name):
    ctx_fields = _SEG_FIELDS if ctx_out else _SEG_FIELDS[1:6]
    lat = [lat[f] for f in _SEG_FIELDS]
    ctx = [ctx[f] for f in ctx_fields]
    B, T, HK = lat[0].shape
    n_ctx = ctx[0].shape[1]
    n_groups = HK // LANES
    wv = hpg * LANES
    unroll = SCAN_UNROLL
    for rows in (T, n_ctx):
        assert rows % CHUNK == 0 and (rows // CHUNK) % min(unroll, rows // CHUNK) == 0
    assert (T // CHUNK) % READOUT_UNROLL == 0
    tri, sg, pm = (np.stack(c) for c in zip(_scan_constants(False), _scan_constants(True)))
    tri = jnp.asarray(tri, BF16)
    sg = jnp.asarray(sg, F32)
    pm = jnp.asarray(pm, F32)

    def seq(rows, widths):
        return [pl.BlockSpec((1, rows, w_), lambda b, j: (b, 0, j)) for w_ in widths]

    width = dict(zip(_SEG_FIELDS, (LANES,) * 5 + (wv, wv)))
    out_rows = (T, n_ctx) if ctx_out else (T,)
    kern = functools.partial(_scan_kernel, hpg=hpg, n_lat_chunks=T // CHUNK, n_ctx_chunks=n_ctx // CHUNK,
                             ctx_out=ctx_out, unroll=unroll)
    return pl.pallas_call(
        kern,
        grid=(B, n_groups),
        in_specs=seq(T, [width[f] for f in _SEG_FIELDS]) + seq(n_ctx, [width[f] for f in ctx_fields])
        + [_const_spec(a.shape) for a in (gain, tri, sg, pm)],
        out_specs=[pl.BlockSpec((1, r, wv), lambda b, j: (b, 0, j)) for r in out_rows],
        out_shape=[jax.ShapeDtypeStruct((B, r, n_groups * wv), BF16) for r in out_rows],
        scratch_shapes=[pltpu.VMEM((T, wv), F32), pltpu.VMEM((T, wv), F32),
                        pltpu.VMEM((n_ctx, wv), F32), pltpu.VMEM((n_ctx, wv), F32),
                        pltpu.VMEM((2, wv, LANES), F32)],
        compiler_params=_params("arbitrary", "arbitrary"),
        name=name,
    )(*lat, *ctx, gain, tri, sg, pm)


def _attn_kernel(q_ref, *refs):
    o_ref = refs[-1]
    kv_refs = [(refs[i], refs[i + 1]) for i in range(0, len(refs) - 1, 2)]
    lane = lax.broadcasted_iota(jnp.int32, (q_ref.shape[1], LANES), 1)
    for grp in range(q_ref.shape[2] // LANES):
        cols = slice(grp * LANES, (grp + 1) * LANES)
        q2 = q_ref[0, :, cols]
        outs = []
        for kv in range(ATT_KV_HEADS):
            qm = jnp.where(lane // ATT_HEAD_DIM == kv, q2, jnp.zeros_like(q2))
            ss = [_dot_nt(qm, k_ref[0]) for k_ref, _ in kv_refs]
            m = functools.reduce(jnp.maximum, [jnp.max(s, axis=-1, keepdims=True) for s in ss])
            pv = None
            for s, (_, v_ref) in zip(ss, kv_refs):
                p = jnp.exp2((s - m).astype(BF16))
                term = _dot(p, v_ref[0, :, kv * LANES:(kv + 1) * LANES])
                pv = term if pv is None else pv + term
            outs.append(pv / pltpu.roll(pv, ATT_HEAD_DIM, 1))
        o_ref[0, :, cols] = jnp.where(lane // ATT_HEAD_DIM == 0, outs[0], outs[1]).astype(o_ref.dtype)


def _attention(aq, kvs, name):
    B, R, W = aq.shape
    tq = min(ATT_Q_TILE, R)
    assert R % tq == 0
    wq = ATT_Q_GROUPS * LANES
    in_specs = [pl.BlockSpec((1, tq, wq), lambda b, i, j: (b, i, j))]
    args = [aq]
    for k, v in kvs:
        in_specs += [pl.BlockSpec((1,) + k.shape[1:], lambda b, i, j: (b, 0, 0)),
                     pl.BlockSpec((1,) + v.shape[1:], lambda b, i, j: (b, 0, 0))]
        args += [k, v]
    return pl.pallas_call(
        _attn_kernel,
        grid=(B, R // tq, W // wq),
        in_specs=in_specs,
        out_specs=pl.BlockSpec((1, tq, wq), lambda b, i, j: (b, i, j)),
        out_shape=jax.ShapeDtypeStruct((B, R, W), BF16),
        compiler_params=_params("arbitrary", "arbitrary", "arbitrary"),
        name=name,
    )(*args)


def _post_kernel(*refs, n_mix):
    x_ref, mod_ref = refs[0], refs[1]
    mix_refs = refs[2:2 + n_mix]
    wout_refs = refs[2 + n_mix:2 + 2 * n_mix]
    g_pm_ref, g_pf_ref, g_pff_ref, wg_ref, wu_ref, wd_ref, o_ref = refs[2 + 2 * n_mix:]
    D = x_ref.shape[-1]
    n_ff = wg_ref.shape[1]
    m = mod_ref[0]
    subs = _sub_tiles(x_ref.shape[1])
    ys = []
    for sl in subs:
        y = _dot(mix_refs[0][0, sl, :], wout_refs[0][...])
        for mr, wr in zip(mix_refs[1:], wout_refs[1:]):
            y = y + _dot(mr[0, sl, :], wr[...])
        ys.append(y)
    x1s = [x_ref[0, sl, :] + m[:, 2 * D:3 * D] * _rms_rows(y, g_pm_ref[...]) for sl, y in zip(subs, ys)]
    hs = [_modulated_norm(x1, m, g_pf_ref[...], 3, D).astype(BF16) for x1 in x1s]
    acts = []
    for h in hs:
        cols = []
        for c0 in range(0, n_ff, FFN_COL_TILE):
            c1 = min(c0 + FFN_COL_TILE, n_ff)
            cols.append((_silu(_dot(h, wg_ref[:, c0:c1])) * _dot(h, wu_ref[:, c0:c1])).astype(BF16))
        acts.append(jnp.concatenate(cols, axis=1))
    for sl, x1, act in zip(subs, x1s, acts):
        z = _dot(act, wd_ref[...])
        o_ref[0, sl, :] = x1 + m[:, 5 * D:6 * D] * _rms_rows(z, g_pff_ref[...])


def _post(x, mods, mod_row, mixes, consts, tm, name):
    D = x.shape[-1]
    return _row_call(functools.partial(_post_kernel, n_mix=len(mixes)), name, x, mods, mod_row, consts, mixes,
                     (D,), (F32,), tm)[0]


def _deinterleave(n_heads):
    within = np.concatenate([np.arange(0, ATT_HEAD_DIM, 2), np.arange(1, ATT_HEAD_DIM, 2)])
    return (np.arange(n_heads)[:, None] * ATT_HEAD_DIM + within[None, :]).reshape(-1)


def _attn_head_order():
    return np.array([kv * ATT_GROUP + j for j in range(ATT_GROUP) for kv in range(ATT_KV_HEADS)])


def _rope_tables(T):
    rows_n = T // GRID_W
    row = jnp.repeat(jnp.arange(rows_n), GRID_W).astype(F32)
    col = jnp.tile(jnp.arange(GRID_W), rows_n).astype(F32)
    axis_dim = ATT_HEAD_DIM // 2
    inv = ROPE_THETA ** (-jnp.arange(0, axis_dim, 2, dtype=F32) / axis_dim)
    ang = jnp.concatenate([row[:, None] * inv, col[:, None] * inv], axis=-1)
    cos, sin = jnp.cos(ang), jnp.sin(ang)
    reps = LANES // ATT_HEAD_DIM
    cs = jnp.tile(jnp.concatenate([cos, cos], axis=1), (1, reps))
    sn = jnp.tile(jnp.concatenate([-sin, sin], axis=1), (1, reps))
    return cs, sn


def _block_mean_matrix(width, group):
    idx = np.arange(width) // group
    return jnp.asarray((idx[:, None] == idx[None, :]).astype(np.float32) / group, BF16)


def kernel(x, c, ctx, c_ctx, mod_w, mod_b, norm_pre_mix, norm_post_mix, norm_pre_ffn, norm_post_ffn, even_w_in,
           gla_w_gate, gla_b_gate, gla_out_norm, att_q_norm, att_k_norm, even_w_out, odd_w_in, hgrn_lower_bounds,
           hgrn_out_norm, odd_w_out, ffn_w_gate, ffn_w_up, ffn_w_down):
    B, T, D = x.shape
    n_ctx = ctx.shape[1]
    depth = mod_w.shape[0]
    tm_lat = LAT_ROW_TILE
    tm_ctx = min(CTX_ROW_TILE, n_ctx)
    assert T % tm_lat == 0 and n_ctx % tm_ctx == 0 and T % GRID_W == 0 and D % LANES == 0

    n_rows = -(-(B + 1) // 8) * 8
    cond = jnp.concatenate([c, c_ctx[None, :], jnp.zeros((n_rows - B - 1, D), F32)], axis=0)
    mods = _modulation(cond, mod_w, mod_b).reshape(depth * n_rows, 1, 6 * D)

    vec = lambda a: a.reshape(1, -1).astype(F32)
    bf = lambda a: a.astype(BF16)
    rope_tables = _rope_tables(T)
    x_lat, x_ctx = x, ctx

    for l in range(depth):
        last = l == depth - 1
        j = l // 2
        lat_row = lambda b, base=l * n_rows: base + b
        ctx_row = lambda b, base=l * n_rows: base + B
        if l % 2 == 0:
            sizes = np.cumsum([0, GLA_HEADS * GLA_DK, GLA_HEADS * GLA_DK, GLA_HEADS * GLA_DV, GLA_HEADS * GLA_DV,
                               2 * GLA_GATE_RANK, ATT_Q_HEADS * ATT_HEAD_DIM, ATT_KV_HEADS * ATT_HEAD_DIM,
                               ATT_KV_HEADS * ATT_HEAD_DIM])
            o_gq, o_gk, o_gv, o_gog, o_gz, o_aq, o_ak, o_av = sizes[:-1]
            q_cols = o_aq + (_attn_head_order()[:, None] * ATT_HEAD_DIM
                             + _deinterleave(1)[None, :]).reshape(-1)
            k_cols = o_ak + _deinterleave(ATT_KV_HEADS)
            cols = np.concatenate([np.arange(o_gq, o_gz), q_cols, k_cols, np.arange(o_av, sizes[-1]),
                                   np.arange(o_gz, o_aq)])
            w_in = bf(even_w_in[j][:, cols])
            wgate = gla_w_gate[j]
            nk = GLA_HEADS * GLA_DK
            zeros = jnp.zeros((GLA_GATE_RANK, nk), F32)
            wg2 = bf(jnp.concatenate([jnp.concatenate([wgate[0], zeros], axis=1),
                                      jnp.concatenate([zeros, wgate[1]], axis=1)], axis=0))
            bg2 = gla_b_gate[j].reshape(1, 2 * nk).astype(F32)
            qn = vec(jnp.tile(att_q_norm[j][_deinterleave(1)], ATT_Q_HEADS))
            kn = vec(jnp.tile(att_k_norm[j][_deinterleave(1)], ATT_KV_HEADS))
            bdq = _block_mean_matrix(ATT_Q_HEADS * ATT_HEAD_DIM, ATT_HEAD_DIM)
            bdk = _block_mean_matrix(ATT_KV_HEADS * ATT_HEAD_DIM, ATT_HEAD_DIM)
            consts = (vec(norm_pre_mix[l]), w_in, wg2, bg2, qn, kn, bdq, bdk)
            lat = _pre_even(x_lat, mods, lat_row, consts, rope_tables, tm_lat)
            cx = _pre_even(x_ctx, mods, ctx_row, consts, None, tm_ctx)
            pick = lambda o: dict(zip(_SEG_FIELDS, (o[0], o[1], o[1], o[4], o[5], o[2], o[3])))
            mix_a = _scan(pick(lat), pick(cx), vec(gla_out_norm[j]), hpg=LANES // GLA_DK, ctx_out=not last,
                          name="scan_gla")
            k_all = jnp.concatenate([cx[7], lat[7]], axis=1)
            v_all = jnp.concatenate([cx[8], lat[8]], axis=1)
            mix_b_lat = _attention(lat[6], [(k_all, v_all)], "attention_lat")
            mixes_lat = (mix_a[0], mix_b_lat)
            if not last:
                mixes_ctx = (mix_a[1], _attention(cx[6], [(cx[7], cx[8])], "attention_ctx"))
            n_a = GLA_HEADS * GLA_DV
            b_rows = n_a + (_attn_head_order()[:, None] * ATT_HEAD_DIM
                            + np.arange(ATT_HEAD_DIM)[None, :]).reshape(-1)
            wouts = (bf(even_w_out[j][:n_a]), bf(even_w_out[j][b_rows]))
        else:
            w_in = bf(odd_w_in[j])
            lbs = hgrn_lower_bounds.astype(F32)
            lat = _pre_odd(x_lat, mods, lat_row, vec(norm_pre_mix[l]), w_in, lbs, l, tm_lat, "pre_odd_lat")
            cx = _pre_odd(x_ctx, mods, ctx_row, vec(norm_pre_mix[l]), w_in, lbs, l, tm_ctx, "pre_odd_ctx",
                          with_q=not last)
            mix = _scan(lat, cx, vec(hgrn_out_norm[j]), hpg=LANES // HGRN_DF, ctx_out=not last, name="scan_hgrn")
            mixes_lat = (mix[0],)
            if not last:
                mixes_ctx = (mix[1],)
            wouts = (bf(odd_w_out[j]),)
        consts = (*wouts, vec(norm_post_mix[l]), vec(norm_pre_ffn[l]), vec(norm_post_ffn[l]),
                  bf(ffn_w_gate[l]), bf(ffn_w_up[l]), bf(ffn_w_down[l]))
        if not last:
            x_ctx = _post(x_ctx, mods, ctx_row, mixes_ctx, consts, tm_ctx, "post_ctx")
        x_lat = _post(x_lat, mods, lat_row, mixes_lat, consts, tm_lat, "post_lat")
    return x_lat
```

```python
import functools

import numpy as np
import jax
import jax.numpy as jnp
from jax import lax
from jax.experimental import pallas as pl
from jax.experimental.pallas import tpu as pltpu

GRID_W = 64
GLA_HEADS = 4
GLA_DK = 64
GLA_DV = 128
GLA_GATE_RANK = 16
GLA_GATE_TAU = 16.0
ATT_Q_HEADS = 8
ATT_KV_HEADS = 2
ATT_HEAD_DIM = 64
ATT_GROUP = ATT_Q_HEADS // ATT_KV_HEADS
ROPE_THETA = 10000.0
HGRN_HEADS = 8
HGRN_DF = 128
EPS = 1e-6

LANES = 128
VMEM_LIMIT_BYTES = 60000 * 1024

LAT_ROW_TILE = 512
CTX_ROW_TILE = 256
SUB_ROWS = 256
MOD_COL_TILE = 2048
FFN_COL_TILE = 256
ATT_Q_TILE = 256
ATT_Q_GROUPS = 4
CHUNK = 128
N_LEVELS = CHUNK.bit_length() - 1
SCAN_UNROLL = 8
READOUT_UNROLL = 4

F32 = jnp.float32
BF16 = jnp.bfloat16
LOG2E = 1.4426950408889634

_SEG_FIELDS = ("q", "k_fwd", "k_bwd", "g_fwd", "g_bwd", "v", "out_gate")


def _dot(a, b):
    return jnp.dot(a, b, preferred_element_type=F32)


def _dot_nt(a, b):
    return lax.dot_general(a, b, (((1,), (1,)), ((), ())), preferred_element_type=F32)


def _dot_tn(a, b):
    return lax.dot_general(a, b, (((0,), (0,)), ((), ())), preferred_element_type=F32)


def _rms_rows(x, gain):
    return x * lax.rsqrt(jnp.mean(x * x, axis=-1, keepdims=True) + EPS) * gain


def _silu(x):
    return x / (1.0 + jnp.exp(-x))


def _log_sigmoid(x):
    return jnp.minimum(x, 0.0) - jnp.log1p(jnp.exp(-jnp.abs(x)))


def _params(*sem):
    return pltpu.CompilerParams(dimension_semantics=sem, vmem_limit_bytes=VMEM_LIMIT_BYTES)


def _const_spec(shape, layer=None):
    if layer is None:
        return pl.BlockSpec(shape, lambda *_: (0,) * len(shape), pipeline_mode=pl.Buffered(1))
    return pl.BlockSpec((None,) + tuple(shape[1:]), lambda *_: (layer,) + (0,) * (len(shape) - 1),
                        pipeline_mode=pl.Buffered(1))


def _sub_tiles(n_rows):
    sub = min(SUB_ROWS, n_rows)
    return [slice(s, s + sub) for s in range(0, n_rows, sub)]


def _mod_kernel(s_ref, w_ref, b_ref, o_ref):
    s = _silu(s_ref[...]).astype(BF16)
    o_ref[0] = _dot(s, w_ref[0].astype(BF16)) + b_ref[0]


def _modulation(cond, mod_w, mod_b):
    L, D, D6 = mod_w.shape
    R = cond.shape[0]
    tn = MOD_COL_TILE
    assert D6 % tn == 0
    return pl.pallas_call(
        _mod_kernel,
        grid=(L, D6 // tn),
        in_specs=[
            pl.BlockSpec((R, D), lambda l, j: (0, 0)),
            pl.BlockSpec((1, D, tn), lambda l, j: (l, 0, j)),
            pl.BlockSpec((1, 1, tn), lambda l, j: (l, 0, j)),
        ],
        out_specs=pl.BlockSpec((1, R, tn), lambda l, j: (l, 0, j)),
        out_shape=jax.ShapeDtypeStruct((L, R, D6), F32),
        compiler_params=_params("arbitrary", "arbitrary"),
        name="modulation",
    )(cond, mod_w, mod_b.reshape(L, 1, D6))


def _modulated_norm(x, mod_row, gain, shift_idx, D):
    shift = mod_row[:, shift_idx * D:(shift_idx + 1) * D]
    scale = mod_row[:, (shift_idx + 1) * D:(shift_idx + 2) * D]
    return _rms_rows(x, gain) * (1.0 + scale) + shift


def _row_call(kernel, name, x, mods, mod_row, consts, row_inputs, out_widths, out_dtypes, tm):
    B, R, D = x.shape
    assert R % tm == 0
    row = lambda w_: pl.BlockSpec((1, tm, w_), lambda b, i: (b, i, 0))
    in_specs = [row(D), pl.BlockSpec((1, 1, mods.shape[-1]), lambda b, i: (mod_row(b), 0, 0))]
    in_specs += [pl.BlockSpec((tm, a.shape[-1]), lambda b, i: (i, 0)) if a.ndim == 2 else row(a.shape[-1])
                 for a in row_inputs]
    in_specs += [_const_spec(a[0].shape, a[1]) if isinstance(a, tuple) else _const_spec(a.shape) for a in consts]
    consts = [a[0] if isinstance(a, tuple) else a for a in consts]
    return pl.pallas_call(
        kernel,
        grid=(B, R // tm),
        in_specs=in_specs,
        out_specs=[row(w_) for w_ in out_widths],
        out_shape=[jax.ShapeDtypeStruct((B, R, w_), dt) for w_, dt in zip(out_widths, out_dtypes)],
        compiler_params=_params("arbitrary", "arbitrary"),
        name=name,
    )(x, mods, *row_inputs, *consts)


_EVEN_WIDTHS = (GLA_HEADS * GLA_DK, GLA_HEADS * GLA_DK, GLA_HEADS * GLA_DV, GLA_HEADS * GLA_DV,
                ATT_Q_HEADS * ATT_HEAD_DIM, ATT_KV_HEADS * ATT_HEAD_DIM, ATT_KV_HEADS * ATT_HEAD_DIM,
                2 * GLA_GATE_RANK)
E_GQ, E_GK, E_GV, E_GOG, E_AQ, E_AK, E_AV, E_GZ, E_END = (int(c) for c in np.cumsum((0,) + _EVEN_WIDTHS))


def _swap_halves(x):
    w = x.shape[-1]
    half = ATT_HEAD_DIM // 2
    lane = lax.broadcasted_iota(jnp.int32, x.shape, x.ndim - 1)
    first_half = (lane % ATT_HEAD_DIM) < half
    return jnp.where(first_half, pltpu.roll(x, w - half, x.ndim - 1), pltpu.roll(x, half, x.ndim - 1))


def _head_norm(y, bd_ref, gain_ref):
    ms = _dot((y * y).astype(BF16), bd_ref[...])
    return y * lax.rsqrt(ms + EPS) * gain_ref[...]


def _rope(y, cs, sn):
    reps = y.shape[-1] // LANES
    cs = jnp.concatenate([cs] * reps, axis=1) if reps > 1 else cs
    sn = jnp.concatenate([sn] * reps, axis=1) if reps > 1 else sn
    return y * cs + _swap_halves(y) * sn


def _pre_even_kernel(*refs, rope):
    x_ref, mod_ref = refs[:2]
    refs = refs[2:]
    if rope:
        cs_ref, sn_ref = refs[:2]
        refs = refs[2:]
    gain_ref, w_ref, wg_ref, bg_ref, qn_ref, kn_ref, bdq_ref, bdk_ref = refs[:8]
    gq_ref, gk_ref, gv_ref, gog_ref, gf_ref, gb_ref, aq_ref, ak_ref, av_ref = refs[8:]
    D = x_ref.shape[-1]
    nk = GLA_HEADS * GLA_DK
    subs = _sub_tiles(x_ref.shape[1])
    hs = [_modulated_norm(x_ref[0, sl, :], mod_ref[0], gain_ref[...], 0, D).astype(BF16) for sl in subs]
    for sl, h in zip(subs, hs):
        acc = _dot(h, w_ref[:, E_GQ:E_GV])
        gq_ref[0, sl, :] = (acc[:, :nk] * (GLA_DK ** -0.5)).astype(BF16)
        gk_ref[0, sl, :] = acc[:, nk:].astype(BF16)
        gv_ref[0, sl, :] = _dot(h, w_ref[:, E_GV:E_GOG]).astype(BF16)
        gog_ref[0, sl, :] = _dot(h, w_ref[:, E_GOG:E_AQ]).astype(BF16)
        aq = _head_norm(_dot(h, w_ref[:, E_AQ:E_AK]), bdq_ref, qn_ref)
        acc = _dot(h, w_ref[:, E_AK:E_END])
        ak = _head_norm(acc[:, :E_AV - E_AK], bdk_ref, kn_ref)
        if rope:
            aq = _rope(aq, cs_ref[sl, :], sn_ref[sl, :])
            ak = _rope(ak, cs_ref[sl, :], sn_ref[sl, :])
        aq_ref[0, sl, :] = (aq * (ATT_HEAD_DIM ** -0.5 * LOG2E)).astype(BF16)
        ak_ref[0, sl, :] = ak.astype(BF16)
        av = acc[:, E_AV - E_AK:E_GZ - E_AK]
        first = lax.broadcasted_iota(jnp.int32, av.shape, 1) < ATT_HEAD_DIM
        av_ref[0, sl, :] = jnp.concatenate([jnp.where(first, av, 1.0), jnp.where(first, 1.0, av)],
                                           axis=1).astype(BF16)
        z = acc[:, E_GZ - E_AK:].astype(BF16)
        g = _log_sigmoid(_dot(z, wg_ref[...]) + bg_ref[...]) * (LOG2E / GLA_GATE_TAU)
        gf_ref[0, sl, :] = g[:, :nk]
        gb_ref[0, sl, :] = g[:, nk:]


def _pre_even(x, mods, mod_row, consts, rope_tables, tm):
    nk = GLA_HEADS * GLA_DK
    widths = (nk, nk, GLA_HEADS * GLA_DV, GLA_HEADS * GLA_DV, nk, nk, ATT_Q_HEADS * ATT_HEAD_DIM,
              ATT_KV_HEADS * ATT_HEAD_DIM, ATT_KV_HEADS * LANES)
    dtypes = (BF16, BF16, BF16, BF16, F32, F32, BF16, BF16, BF16)
    rope = rope_tables is not None
    return _row_call(functools.partial(_pre_even_kernel, rope=rope), "pre_even_lat" if rope else "pre_even_ctx",
                     x, mods, mod_row, consts, rope_tables if rope else (), widths, dtypes, tm)


def _pre_odd_kernel(x_ref, mod_ref, gain_ref, w_ref, lb_ref, *out_refs, layer, with_q):
    if with_q:
        q_ref, kf_ref, kb_ref, gf_ref, gb_ref, v_ref, og_ref = out_refs
    else:
        kf_ref, kb_ref, gf_ref, gb_ref, v_ref = out_refs
    D = x_ref.shape[-1]
    F = v_ref.shape[-1]
    subs = _sub_tiles(x_ref.shape[1])
    hs = [_modulated_norm(x_ref[0, sl, :], mod_ref[0], gain_ref[...], 0, D).astype(BF16) for sl in subs]
    lbs = []
    for d in range(2):
        raw = lb_ref[d]
        p = jnp.exp(raw - jnp.max(raw, axis=0, keepdims=True))
        p = p / jnp.sum(p, axis=0, keepdims=True)
        lbs.append(jnp.sum(p[1:layer + 1], axis=0, keepdims=True) if layer > 0 else jnp.zeros_like(p[0:1]))
    for sl, h in zip(subs, hs):
        if with_q:
            q_ref[0, sl, :] = _silu(_dot(h, w_ref[:, 0:F])).astype(BF16)
        for d, (k_ref, g_ref) in enumerate(((kf_ref, gf_ref), (kb_ref, gb_ref))):
            lb = lbs[d]
            x = _dot(h, w_ref[:, (1 + d) * F:(2 + d) * F])
            t = jnp.exp(-jnp.abs(x))
            r = 1.0 / (1.0 + t)
            tr = t * r
            sig_pos = jnp.where(x >= 0, r, tr)
            sig_neg = jnp.where(x >= 0, tr, r)
            g_ref[0, sl, :] = jnp.log2(lb + (1.0 - lb) * sig_pos)
            k_ref[0, sl, :] = ((1.0 - lb) * sig_neg).astype(BF16)
        v_ref[0, sl, :] = _dot(h, w_ref[:, 3 * F:4 * F]).astype(BF16)
        if with_q:
            og_ref[0, sl, :] = _dot(h, w_ref[:, 4 * F:5 * F]).astype(BF16)


def _pre_odd(x, mods, mod_row, gain, w, lower_bounds, layer, tm, name, with_q=True):
    F = w.shape[1] // 5
    fields = _SEG_FIELDS if with_q else _SEG_FIELDS[1:6]
    dtype = dict(zip(_SEG_FIELDS, (BF16, BF16, BF16, F32, F32, BF16, BF16)))
    outs = _row_call(functools.partial(_pre_odd_kernel, layer=layer, with_q=with_q), name, x, mods, mod_row,
                     (gain, w, lower_bounds), (), (F,) * len(fields), [dtype[f] for f in fields], tm)
    return dict(zip(fields, outs))


def _scan_constants(reverse):
    C = CHUNK
    i = np.arange(C)[:, None]
    t = np.arange(C)[None, :]
    tri = (t >= i) if reverse else (t <= i)
    sg, pm = [], []
    size = C
    while size >= 2:
        half = size // 2
        a = (i // size) * size + half
        a_t = (t // size) * size + half
        same = (i // size) == (t // size)
        qside = (i < a) if reverse else (i >= a)
        sg.append(np.broadcast_to(np.where(qside, 1.0, -1.0), (C, LANES)))
        pm.append(same & qside & ((t >= a_t) if reverse else (t < a_t)))
        size = half
    cat = lambda xs: np.concatenate(xs, 0).astype(np.float32)
    return tri.astype(np.float32), cat(sg), cat(pm)


def _anchor_rows(G, size):
    C = G.shape[0]
    half = size // 2
    if size >= 8:
        parts = [jnp.broadcast_to(G[b * size + half:b * size + half + 1, :], (size, G.shape[1]))
                 for b in range(C // size)]
        return jnp.concatenate(parts, axis=0) if len(parts) > 1 else parts[0]
    row = lax.broadcasted_iota(jnp.int32, G.shape, 0) % size
    out = G
    for r in range(size):
        if r != half:
            out = jnp.where(row == r, pltpu.roll(G, (C - (half - r)) % C, 0), out)
    return out


def _scan_kernel(*refs, hpg, n_lat_chunks, n_ctx_chunks, ctx_out, unroll):
    def segment(seg_refs):
        with_q = len(seg_refs) == len(_SEG_FIELDS)
        rest = seg_refs[1:6] if with_q else seg_refs
        return dict(q=seg_refs[0] if with_q else None, k=rest[0:2], g=rest[2:4], v=rest[4],
                    og=seg_refs[6] if with_q else None)

    n_ctx_refs = len(_SEG_FIELDS) if ctx_out else len(_SEG_FIELDS) - 2
    lat_refs = segment(refs[:len(_SEG_FIELDS)])
    refs = refs[len(_SEG_FIELDS):]
    ctx_refs = segment(refs[:n_ctx_refs])
    gain_ref, tri_ref, sg_ref, pm_ref = refs[n_ctx_refs:n_ctx_refs + 4]
    refs = refs[n_ctx_refs + 4:]
    n_out = 2 if ctx_out else 1
    out_refs = refs[:n_out]
    of_lat, ob_lat, of_ctx, ob_ctx, st_scr = refs[n_out:]
    C = CHUNK
    K = LANES // hpg
    DV = LANES

    st_scr[...] = jnp.zeros_like(st_scr)
    lane = lax.broadcasted_iota(jnp.int32, (C, LANES), 1)
    row = lax.broadcasted_iota(jnp.int32, (C, LANES), 0)
    st_row = lax.broadcasted_iota(jnp.int32, (hpg * DV, LANES), 0)
    st_lane = lax.broadcasted_iota(jnp.int32, (hpg * DV, LANES), 1)
    st_mask = (st_row // DV) == (st_lane // K)

    def load(seg, d, row_starts):
        q_ref, v_ref, k_ref, g_ref = seg["q"], seg["v"], seg["k"][d], seg["g"][d]
        gs = [g_ref[0, pl.ds(r0, C), :] for r0 in row_starts]
        parts = []
        for g in gs:
            g_hi = g.astype(BF16)
            parts += [g_hi, (g - g_hi.astype(F32)).astype(BF16)]
        r = _dot(tri_ref[d], jnp.concatenate(parts, axis=1))
        return [dict(d=d, r0=r0, g=g, cum=r[:, 2 * i * LANES:(2 * i + 1) * LANES]
                     + r[:, (2 * i + 1) * LANES:(2 * i + 2) * LANES],
                     q=None if q_ref is None else q_ref[0, pl.ds(r0, C), :].astype(F32),
                     k=k_ref[0, pl.ds(r0, C), :].astype(F32),
                     v=v_ref[0, pl.ds(r0, C), :]) for i, (r0, g) in enumerate(zip(row_starts, gs))]

    def level_scores(it, l):
        d, cum, g = it["d"], it["cum"], it["g"]
        size = C >> l
        half = size // 2
        later_half = (row & half) != 0
        qside = jnp.logical_not(later_half) if d == 1 else later_half
        nxt = lambda n: pltpu.roll(g, C - n, 0)
        if size == 2:
            delta = jnp.where((row & 1) == 0, g if d == 1 else nxt(1), 0.0)
        elif size == 4:
            r4 = row & 3
            if d == 1:
                r0_, r1_, r3_ = g + nxt(1), g, pltpu.roll(g, 1, 0)
            else:
                r0_, r1_, r3_ = nxt(1) + nxt(2), nxt(1), g
            delta = jnp.where(r4 == 0, r0_, jnp.where(r4 == 1, r1_, jnp.where(r4 == 2, 0.0, r3_)))
        else:
            delta = (cum - _anchor_rows(cum, size)) * sg_ref[d, l * C:(l + 1) * C, :]
        t = jnp.where(qside, it["q"], it["k"]) * jnp.exp2(delta)
        pm = pm_ref[d, l * C:(l + 1) * C, :]
        if hpg == 1:
            return _dot_nt(t.astype(BF16), t.astype(BF16)) * pm
        lhs = jnp.concatenate([jnp.where(lane // K == hh, t, 0.0) for hh in range(hpg)], axis=0)
        return _dot_nt(lhs.astype(BF16), t.astype(BF16)) * jnp.concatenate([pm] * hpg, axis=0)

    def write_output(it, acc, st, o_scr):
        cum, q, k, v = it["cum"], it["q"], it["k"], it["v"]
        qk = q * k
        inter = _dot_nt((q * jnp.exp2(cum)).astype(BF16), st.astype(BF16))
        outs = []
        for hh in range(hpg):
            v_h = v[:, hh * DV:(hh + 1) * DV]
            diag = jnp.sum(qk if hpg == 1 else jnp.where(lane // K == hh, qk, 0.0), axis=1, keepdims=True)
            a_h = acc[hh * C:(hh + 1) * C].astype(BF16)
            outs.append(_dot(a_h, v_h) + diag * v_h.astype(F32) + inter[:, hh * DV:(hh + 1) * DV])
        o_scr[pl.ds(it["r0"], C), :] = jnp.concatenate(outs, axis=1) if hpg > 1 else outs[0]

    def next_state(it, st):
        d, cum = it["d"], it["cum"]
        edge = cum[0:1] if d == 1 else cum[C - 1:C]
        ks = (it["k"] * jnp.exp2(edge - cum)).astype(BF16)
        upd = _dot_tn(it["v"], ks)
        if hpg > 1:
            upd = jnp.where(st_mask, upd, 0.0)
        return st * jnp.exp2(edge) + upd

    def trip(seg, o_scrs, n_seg, t, n_u):
        start = lambda c: c * C if isinstance(c, int) else pl.multiple_of(c * C, C)
        fwd = load(seg, 0, [start(t * n_u + u) for u in range(n_u)])
        bwd = load(seg, 1, [start(n_seg - 1 - (t * n_u + u)) for u in range(n_u)])
        sts = [st_scr[0], st_scr[1]]
        for u in range(n_u):
            pair = [fwd[u], bwd[u]]
            if o_scrs is not None:
                accs = [level_scores(it, 0) for it in pair]
                for l in range(1, N_LEVELS):
                    accs = [acc + level_scores(it, l) for it, acc in zip(pair, accs)]
            for i, it in enumerate(pair):
                if o_scrs is not None:
                    write_output(it, accs[i], sts[it["d"]], o_scrs[it["d"]])
                sts[it["d"]] = next_state(it, sts[it["d"]])
        st_scr[0] = sts[0]
        st_scr[1] = sts[1]

    def run_segment(seg, o_scrs, n_seg):
        n_u = min(unroll, n_seg)
        n_trips = n_seg // n_u
        if n_trips == 1:
            trip(seg, o_scrs, n_seg, 0, n_u)
        else:
            def body(t, carry):
                trip(seg, o_scrs, n_seg, t, n_u)
                return carry
            lax.fori_loop(0, n_trips, body, 0)

    def readout_chunks(o_scrs, og_ref, out_ref, c0, n):
        for u in range(n):
            c = c0 + u
            r0 = c * C if isinstance(c, int) else pl.multiple_of(c * C, C)
            o = o_scrs[0][pl.ds(r0, C), :] + o_scrs[1][pl.ds(r0, C), :]
            gate = _silu(og_ref[0, pl.ds(r0, C), :].astype(F32))
            outs = [_rms_rows(o[:, hh * DV:(hh + 1) * DV], gain_ref[...]) * gate[:, hh * DV:(hh + 1) * DV]
                    for hh in range(hpg)]
            y = jnp.concatenate(outs, axis=1) if hpg > 1 else outs[0]
            out_ref[0, pl.ds(r0, C), :] = y.astype(out_ref.dtype)

    def readout(o_scrs, og_ref, out_ref, n_seg):
        ru = min(READOUT_UNROLL, n_seg)
        if n_seg == ru:
            readout_chunks(o_scrs, og_ref, out_ref, 0, ru)
        else:
            def body(t, carry):
                readout_chunks(o_scrs, og_ref, out_ref, t * ru, ru)
                return carry
            lax.fori_loop(0, n_seg // ru, body, 0)

    run_segment(ctx_refs, (of_ctx, ob_ctx) if ctx_out else None, n_ctx_chunks)
    run_segment(lat_refs, (of_lat, ob_lat), n_lat_chunks)
    readout((of_lat, ob_lat), lat_refs["og"], out_refs[0], n_lat_chunks)
    if ctx_out:
        readout((of_ctx, ob_ctx), ctx_refs["og"], out_refs[1], n_ctx_chunks)


def _scan(lat, ctx, gain, *, hpg, ctx_out, name):
    ctx_fields = _SEG_FIELDS if ctx_out else _SEG_FIELDS[1:6]
    lat = [lat[f] for f in _SEG_FIELDS]
    ctx = [ctx[f] for f in ctx_fields]
    B, T, HK = lat[0].shape
    n_ctx = ctx[0].shape[1]
    n_groups = HK // LANES
    wv = hpg * LANES
    unroll = SCAN_UNROLL
    for rows in (T, n_ctx):
        assert rows % CHUNK == 0 and (rows // CHUNK) % min(unroll, rows // CHUNK) == 0
    assert (T // CHUNK) % READOUT_UNROLL == 0
    tri, sg, pm = (np.stack(c) for c in zip(_scan_constants(False), _scan_constants(True)))
    tri = jnp.asarray(tri, BF16)
    sg = jnp.asarray(sg, F32)
    pm = jnp.asarray(pm, F32)

    def seq(rows, widths):
        return [pl.BlockSpec((1, rows, w_), lambda b, j: (b, 0, j)) for w_ in widths]

    width = dict(zip(_SEG_FIELDS, (LANES,) * 5 + (wv, wv)))
    out_rows = (T, n_ctx) if ctx_out else (T,)
    kern = functools.partial(_scan_kernel, hpg=hpg, n_lat_chunks=T // CHUNK, n_ctx_chunks=n_ctx // CHUNK,
                             ctx_out=ctx_out, unroll=unroll)
    return pl.pallas_call(
        kern,
        grid=(B, n_groups),
        in_specs=seq(T, [width[f] for f in _SEG_FIELDS]) + seq(n_ctx, [width[f] for f in ctx_fields])
        + [_const_spec(a.shape) for a in (gain, tri, sg, pm)],
        out_specs=[pl.BlockSpec((1, r, wv), lambda b, j: (b, 0, j)) for r in out_rows],
        out_shape=[jax.ShapeDtypeStruct((B, r, n_groups * wv), BF16) for r in out_rows],
        scratch_shapes=[pltpu.VMEM((T, wv), F32), pltpu.VMEM((T, wv), F32),
                        pltpu.VMEM((n_ctx, wv), F32), pltpu.VMEM((n_ctx, wv), F32),
                        pltpu.VMEM((2, wv, LANES), F32)],
        compiler_params=_params("arbitrary", "arbitrary"),
        name=name,
    )(*lat, *ctx, gain, tri, sg, pm)


def _attn_kernel(q_ref, *refs, n_seg):
    o_ref = refs[2 * n_seg]
    if n_seg == 1:
        keys = lambda: refs[0][0]
        vals = lambda kv: refs[1][0, :, kv * LANES:(kv + 1) * LANES]
    else:
        k_all, v_all = refs[2 * n_seg + 1:]

        @pl.when((pl.program_id(1) == 0) & (pl.program_id(2) == 0))
        def _():
            r = 0
            for s in range(n_seg):
                n = refs[2 * s].shape[1]
                k_all[r:r + n, :] = refs[2 * s][0]
                v_all[r:r + n, :] = refs[2 * s + 1][0]
                r += n

        keys = lambda: k_all[...]
        vals = lambda kv: v_all[:, kv * LANES:(kv + 1) * LANES]
    lane = lax.broadcasted_iota(jnp.int32, (q_ref.shape[1], LANES), 1)
    for grp in range(q_ref.shape[2] // LANES):
        cols = slice(grp * LANES, (grp + 1) * LANES)
        q2 = q_ref[0, :, cols]
        outs = []
        for kv in range(ATT_KV_HEADS):
            qm = jnp.where(lane // ATT_HEAD_DIM == kv, q2, jnp.zeros_like(q2))
            s = _dot_nt(qm, keys())
            p = jnp.exp2((s - jnp.max(s, axis=-1, keepdims=True)).astype(BF16))
            pv = _dot(p, vals(kv))
            outs.append(pv / pltpu.roll(pv, ATT_HEAD_DIM, 1))
        o_ref[0, :, cols] = jnp.where(lane // ATT_HEAD_DIM == 0, outs[0], outs[1]).astype(o_ref.dtype)


def _attention(aq, kvs, name):
    B, R, W = aq.shape
    tq = min(ATT_Q_TILE, R)
    assert R % tq == 0
    wq = ATT_Q_GROUPS * LANES
    in_specs = [pl.BlockSpec((1, tq, wq), lambda b, i, j: (b, i, j))]
    args = [aq]
    for k, v in kvs:
        in_specs += [pl.BlockSpec((1,) + k.shape[1:], lambda b, i, j: (b, 0, 0)),
                     pl.BlockSpec((1,) + v.shape[1:], lambda b, i, j: (b, 0, 0))]
        args += [k, v]
    n_keys = sum(k.shape[1] for k, _ in kvs)
    scratch = [] if len(kvs) == 1 else [pltpu.VMEM((n_keys, kvs[0][0].shape[2]), BF16),
                                        pltpu.VMEM((n_keys, kvs[0][1].shape[2]), BF16)]
    return pl.pallas_call(
        functools.partial(_attn_kernel, n_seg=len(kvs)),
        grid=(B, R // tq, W // wq),
        in_specs=in_specs,
        out_specs=pl.BlockSpec((1, tq, wq), lambda b, i, j: (b, i, j)),
        out_shape=jax.ShapeDtypeStruct((B, R, W), BF16),
        scratch_shapes=scratch,
        compiler_params=_params("arbitrary", "arbitrary", "arbitrary"),
        name=name,
    )(*args)


def _post_kernel(*refs, n_mix):
    x_ref, mod_ref = refs[0], refs[1]
    mix_refs = refs[2:2 + n_mix]
    wout_refs = refs[2 + n_mix:2 + 2 * n_mix]
    g_pm_ref, g_pf_ref, g_pff_ref, wg_ref, wu_ref, wd_ref, o_ref = refs[2 + 2 * n_mix:]
    D = x_ref.shape[-1]
    n_ff = wg_ref.shape[1]
    m = mod_ref[0]
    subs = _sub_tiles(x_ref.shape[1])
    ys = []
    for sl in subs:
        y = _dot(mix_refs[0][0, sl, :], wout_refs[0][...])
        for mr, wr in zip(mix_refs[1:], wout_refs[1:]):
            y = y + _dot(mr[0, sl, :], wr[...])
        ys.append(y)
    x1s = [x_ref[0, sl, :] + m[:, 2 * D:3 * D] * _rms_rows(y, g_pm_ref[...]) for sl, y in zip(subs, ys)]
    hs = [_modulated_norm(x1, m, g_pf_ref[...], 3, D).astype(BF16) for x1 in x1s]
    acts = []
    for h in hs:
        cols = []
        for c0 in range(0, n_ff, FFN_COL_TILE):
            c1 = min(c0 + FFN_COL_TILE, n_ff)
            cols.append((_silu(_dot(h, wg_ref[:, c0:c1])) * _dot(h, wu_ref[:, c0:c1])).astype(BF16))
        acts.append(jnp.concatenate(cols, axis=1))
    for sl, x1, act in zip(subs, x1s, acts):
        z = _dot(act, wd_ref[...])
        o_ref[0, sl, :] = x1 + m[:, 5 * D:6 * D] * _rms_rows(z, g_pff_ref[...])


def _post(x, mods, mod_row, mixes, consts, tm, name):
    D = x.shape[-1]
    return _row_call(functools.partial(_post_kernel, n_mix=len(mixes)), name, x, mods, mod_row, consts, mixes,
                     (D,), (F32,), tm)[0]


def _deinterleave(n_heads):
    within = np.concatenate([np.arange(0, ATT_HEAD_DIM, 2), np.arange(1, ATT_HEAD_DIM, 2)])
    return (np.arange(n_heads)[:, None] * ATT_HEAD_DIM + within[None, :]).reshape(-1)


def _attn_head_order():
    return np.array([kv * ATT_GROUP + j for j in range(ATT_GROUP) for kv in range(ATT_KV_HEADS)])


def _rope_tables(T):
    rows_n = T // GRID_W
    row = jnp.repeat(jnp.arange(rows_n), GRID_W).astype(F32)
    col = jnp.tile(jnp.arange(GRID_W), rows_n).astype(F32)
    axis_dim = ATT_HEAD_DIM // 2
    inv = ROPE_THETA ** (-jnp.arange(0, axis_dim, 2, dtype=F32) / axis_dim)
    ang = jnp.concatenate([row[:, None] * inv, col[:, None] * inv], axis=-1)
    cos, sin = jnp.cos(ang), jnp.sin(ang)
    reps = LANES // ATT_HEAD_DIM
    cs = jnp.tile(jnp.concatenate([cos, cos], axis=1), (1, reps))
    sn = jnp.tile(jnp.concatenate([-sin, sin], axis=1), (1, reps))
    return cs, sn


def _block_mean_matrix(width, group):
    idx = np.arange(width) // group
    return jnp.asarray((idx[:, None] == idx[None, :]).astype(np.float32) / group, BF16)


def kernel(x, c, ctx, c_ctx, mod_w, mod_b, norm_pre_mix, norm_post_mix, norm_pre_ffn, norm_post_ffn, even_w_in,
           gla_w_gate, gla_b_gate, gla_out_norm, att_q_norm, att_k_norm, even_w_out, odd_w_in, hgrn_lower_bounds,
           hgrn_out_norm, odd_w_out, ffn_w_gate, ffn_w_up, ffn_w_down):
    B, T, D = x.shape
    n_ctx = ctx.shape[1]
    depth = mod_w.shape[0]
    tm_lat = LAT_ROW_TILE
    tm_ctx = min(CTX_ROW_TILE, n_ctx)
    assert T % tm_lat == 0 and n_ctx % tm_ctx == 0 and T % GRID_W == 0 and D % LANES == 0

    n_rows = -(-(B + 1) // 8) * 8
    cond = jnp.concatenate([c, c_ctx[None, :], jnp.zeros((n_rows - B - 1, D), F32)], axis=0)
    mods = _modulation(cond, mod_w, mod_b).reshape(depth * n_rows, 1, 6 * D)

    vec = lambda a: a.reshape(1, -1).astype(F32)
    bf = lambda a: a.astype(BF16)
    rope_tables = _rope_tables(T)
    ffn_gate16, ffn_up16, ffn_down16 = bf(ffn_w_gate), bf(ffn_w_up), bf(ffn_w_down)
    x_lat, x_ctx = x, ctx

    for l in range(depth):
        last = l == depth - 1
        j = l // 2
        lat_row = lambda b, base=l * n_rows: base + b
        ctx_row = lambda b, base=l * n_rows: base + B
        if l % 2 == 0:
            sizes = np.cumsum([0, GLA_HEADS * GLA_DK, GLA_HEADS * GLA_DK, GLA_HEADS * GLA_DV, GLA_HEADS * GLA_DV,
                               2 * GLA_GATE_RANK, ATT_Q_HEADS * ATT_HEAD_DIM, ATT_KV_HEADS * ATT_HEAD_DIM,
                               ATT_KV_HEADS * ATT_HEAD_DIM])
            o_gq, o_gk, o_gv, o_gog, o_gz, o_aq, o_ak, o_av = sizes[:-1]
            q_cols = o_aq + (_attn_head_order()[:, None] * ATT_HEAD_DIM
                             + _deinterleave(1)[None, :]).reshape(-1)
            k_cols = o_ak + _deinterleave(ATT_KV_HEADS)
            cols = np.concatenate([np.arange(o_gq, o_gz), q_cols, k_cols, np.arange(o_av, sizes[-1]),
                                   np.arange(o_gz, o_aq)])
            w_in = bf(even_w_in[j][:, cols])
            wgate = gla_w_gate[j]
            nk = GLA_HEADS * GLA_DK
            zeros = jnp.zeros((GLA_GATE_RANK, nk), F32)
            wg2 = bf(jnp.concatenate([jnp.concatenate([wgate[0], zeros], axis=1),
                                      jnp.concatenate([zeros, wgate[1]], axis=1)], axis=0))
            bg2 = gla_b_gate[j].reshape(1, 2 * nk).astype(F32)
            qn = vec(jnp.tile(att_q_norm[j][_deinterleave(1)], ATT_Q_HEADS))
            kn = vec(jnp.tile(att_k_norm[j][_deinterleave(1)], ATT_KV_HEADS))
            bdq = _block_mean_matrix(ATT_Q_HEADS * ATT_HEAD_DIM, ATT_HEAD_DIM)
            bdk = _block_mean_matrix(ATT_KV_HEADS * ATT_HEAD_DIM, ATT_HEAD_DIM)
            consts = (vec(norm_pre_mix[l]), w_in, wg2, bg2, qn, kn, bdq, bdk)
            lat = _pre_even(x_lat, mods, lat_row, consts, rope_tables, tm_lat)
            cx = _pre_even(x_ctx, mods, ctx_row, consts, None, tm_ctx)
            pick = lambda o: dict(zip(_SEG_FIELDS, (o[0], o[1], o[1], o[4], o[5], o[2], o[3])))
            mix_a = _scan(pick(lat), pick(cx), vec(gla_out_norm[j]), hpg=LANES // GLA_DK, ctx_out=not last,
                          name="scan_gla")
            mix_b_lat = _attention(lat[6], [(cx[7], cx[8]), (lat[7], lat[8])], "attention_lat")
            mixes_lat = (mix_a[0], mix_b_lat)
            if not last:
                mixes_ctx = (mix_a[1], _attention(cx[6], [(cx[7], cx[8])], "attention_ctx"))
            n_a = GLA_HEADS * GLA_DV
            b_rows = n_a + (_attn_head_order()[:, None] * ATT_HEAD_DIM
                            + np.arange(ATT_HEAD_DIM)[None, :]).reshape(-1)
            wouts = (bf(even_w_out[j][:n_a]), bf(even_w_out[j][b_rows]))
        else:
            w_in = bf(odd_w_in[j])
            lbs = hgrn_lower_bounds.astype(F32)
            lat = _pre_odd(x_lat, mods, lat_row, vec(norm_pre_mix[l]), w_in, lbs, l, tm_lat, "pre_odd_lat")
            cx = _pre_odd(x_ctx, mods, ctx_row, vec(norm_pre_mix[l]), w_in, lbs, l, tm_ctx, "pre_odd_ctx",
                          with_q=not last)
            mix = _scan(lat, cx, vec(hgrn_out_norm[j]), hpg=LANES // HGRN_DF, ctx_out=not last, name="scan_hgrn")
            mixes_lat = (mix[0],)
            if not last:
                mixes_ctx = (mix[1],)
            wouts = (bf(odd_w_out[j]),)
        consts = (*wouts, vec(norm_post_mix[l]), vec(norm_pre_ffn[l]), vec(norm_post_ffn[l]),
                  (ffn_gate16, l), (ffn_up16, l), (ffn_down16, l))
        if not last:
            x_ctx = _post(x_ctx, mods, ctx_row, mixes_ctx, consts, tm_ctx, "post_ctx")
        x_lat = _post(x_lat, mods, lat_row, mixes_lat, consts, tm_lat, "post_lat")
    return x_lat
```

```python
import functools

import numpy as np
import jax
import jax.numpy as jnp
from jax import lax
from jax.experimental import pallas as pl
from jax.experimental.pallas import tpu as pltpu

GRID_W = 64
GLA_HEADS = 4
GLA_DK = 64
GLA_DV = 128
GLA_GATE_RANK = 16
GLA_GATE_TAU = 16.0
ATT_Q_HEADS = 8
ATT_KV_HEADS = 2
ATT_HEAD_DIM = 64
ATT_GROUP = ATT_Q_HEADS // ATT_KV_HEADS
ROPE_THETA = 10000.0
HGRN_HEADS = 8
HGRN_DF = 128
EPS = 1e-6

LANES = 128
VMEM_LIMIT_BYTES = 60000 * 1024

LAT_ROW_TILE = 512
POST_ROW_TILE = 1024
CTX_ROW_TILE = 256
SUB_ROWS = 256
MOD_COL_TILE = 2048
FFN_COL_TILE = 256
ATT_Q_TILE = 256
ATT_Q_GROUPS = 4
CHUNK = 128
N_LEVELS = CHUNK.bit_length() - 1
SCAN_UNROLL = 8
READOUT_UNROLL = 4

F32 = jnp.float32
BF16 = jnp.bfloat16
LOG2E = 1.4426950408889634

_SEG_FIELDS = ("q", "k_fwd", "k_bwd", "g_fwd", "g_bwd", "v", "out_gate")


def _dot(a, b):
    return jnp.dot(a, b, preferred_element_type=F32)


def _dot_nt(a, b):
    return lax.dot_general(a, b, (((1,), (1,)), ((), ())), preferred_element_type=F32)


def _dot_tn(a, b):
    return lax.dot_general(a, b, (((0,), (0,)), ((), ())), preferred_element_type=F32)


def _rms_rows(x, gain):
    return x * lax.rsqrt(jnp.mean(x * x, axis=-1, keepdims=True) + EPS) * gain


def _silu(x):
    return x / (1.0 + jnp.exp(-x))


def _log_sigmoid(x):
    return jnp.minimum(x, 0.0) - jnp.log1p(jnp.exp(-jnp.abs(x)))


def _params(*sem):
    return pltpu.CompilerParams(dimension_semantics=sem, vmem_limit_bytes=VMEM_LIMIT_BYTES)


def _const_spec(shape, layer=None):
    if layer is None:
        return pl.BlockSpec(shape, lambda *_: (0,) * len(shape), pipeline_mode=pl.Buffered(1))
    return pl.BlockSpec((None,) + tuple(shape[1:]), lambda *_: (layer,) + (0,) * (len(shape) - 1),
                        pipeline_mode=pl.Buffered(1))


def _sub_tiles(n_rows):
    sub = min(SUB_ROWS, n_rows)
    return [slice(s, s + sub) for s in range(0, n_rows, sub)]


def _mod_kernel(s_ref, w_ref, b_ref, o_ref):
    s = _silu(s_ref[...]).astype(BF16)
    o_ref[0] = _dot(s, w_ref[0].astype(BF16)) + b_ref[0]


def _modulation(cond, mod_w, mod_b):
    L, D, D6 = mod_w.shape
    R = cond.shape[0]
    tn = MOD_COL_TILE
    assert D6 % tn == 0
    return pl.pallas_call(
        _mod_kernel,
        grid=(L, D6 // tn),
        in_specs=[
            pl.BlockSpec((R, D), lambda l, j: (0, 0)),
            pl.BlockSpec((1, D, tn), lambda l, j: (l, 0, j)),
            pl.BlockSpec((1, 1, tn), lambda l, j: (l, 0, j)),
        ],
        out_specs=pl.BlockSpec((1, R, tn), lambda l, j: (l, 0, j)),
        out_shape=jax.ShapeDtypeStruct((L, R, D6), F32),
        compiler_params=_params("arbitrary", "arbitrary"),
        name="modulation",
    )(cond, mod_w, mod_b.reshape(L, 1, D6))


def _modulated_norm(x, mod_row, gain, shift_idx, D):
    shift = mod_row[:, shift_idx * D:(shift_idx + 1) * D]
    scale = mod_row[:, (shift_idx + 1) * D:(shift_idx + 2) * D]
    return _rms_rows(x, gain) * (1.0 + scale) + shift


def _row_call(kernel, name, x, mods, mod_row, consts, row_inputs, out_widths, out_dtypes, tm):
    B, R, D = x.shape
    assert R % tm == 0
    row = lambda w_: pl.BlockSpec((1, tm, w_), lambda b, i: (b, i, 0))
    in_specs = [row(D), pl.BlockSpec((1, 1, mods.shape[-1]), lambda b, i: (mod_row(b), 0, 0))]
    in_specs += [pl.BlockSpec((tm, a.shape[-1]), lambda b, i: (i, 0)) if a.ndim == 2 else row(a.shape[-1])
                 for a in row_inputs]
    in_specs += [_const_spec(a[0].shape, a[1]) if isinstance(a, tuple) else _const_spec(a.shape) for a in consts]
    consts = [a[0] if isinstance(a, tuple) else a for a in consts]
    return pl.pallas_call(
        kernel,
        grid=(B, R // tm),
        in_specs=in_specs,
        out_specs=[row(w_) for w_ in out_widths],
        out_shape=[jax.ShapeDtypeStruct((B, R, w_), dt) for w_, dt in zip(out_widths, out_dtypes)],
        compiler_params=_params("arbitrary", "arbitrary"),
        name=name,
    )(x, mods, *row_inputs, *consts)


_EVEN_WIDTHS = (GLA_HEADS * GLA_DK, GLA_HEADS * GLA_DK, GLA_HEADS * GLA_DV, GLA_HEADS * GLA_DV,
                ATT_Q_HEADS * ATT_HEAD_DIM, ATT_KV_HEADS * ATT_HEAD_DIM, ATT_KV_HEADS * ATT_HEAD_DIM,
                2 * GLA_GATE_RANK)
E_GQ, E_GK, E_GV, E_GOG, E_AQ, E_AK, E_AV, E_GZ, E_END = (int(c) for c in np.cumsum((0,) + _EVEN_WIDTHS))


def _swap_halves(x):
    w = x.shape[-1]
    half = ATT_HEAD_DIM // 2
    lane = lax.broadcasted_iota(jnp.int32, x.shape, x.ndim - 1)
    first_half = (lane % ATT_HEAD_DIM) < half
    return jnp.where(first_half, pltpu.roll(x, w - half, x.ndim - 1), pltpu.roll(x, half, x.ndim - 1))


def _head_norm(y, bd_ref, gain_ref):
    ms = _dot((y * y).astype(BF16), bd_ref[...])
    return y * lax.rsqrt(ms + EPS) * gain_ref[...]


def _rope(y, cs, sn):
    reps = y.shape[-1] // LANES
    cs = jnp.concatenate([cs] * reps, axis=1) if reps > 1 else cs
    sn = jnp.concatenate([sn] * reps, axis=1) if reps > 1 else sn
    return y * cs + _swap_halves(y) * sn


def _pre_even_kernel(*refs, rope):
    x_ref, mod_ref = refs[:2]
    refs = refs[2:]
    if rope:
        cs_ref, sn_ref = refs[:2]
        refs = refs[2:]
    gain_ref, w_ref, wg_ref, bg_ref, qn_ref, kn_ref, bdq_ref, bdk_ref = refs[:8]
    gq_ref, gk_ref, gv_ref, gog_ref, gf_ref, gb_ref, aq_ref, ak_ref, av_ref = refs[8:]
    D = x_ref.shape[-1]
    nk = GLA_HEADS * GLA_DK
    subs = _sub_tiles(x_ref.shape[1])
    hs = [_modulated_norm(x_ref[0, sl, :], mod_ref[0], gain_ref[...], 0, D).astype(BF16) for sl in subs]
    for sl, h in zip(subs, hs):
        acc = _dot(h, w_ref[:, E_GQ:E_GV])
        gq_ref[0, sl, :] = (acc[:, :nk] * (GLA_DK ** -0.5)).astype(BF16)
        gk_ref[0, sl, :] = acc[:, nk:].astype(BF16)
        gv_ref[0, sl, :] = _dot(h, w_ref[:, E_GV:E_GOG]).astype(BF16)
        gog_ref[0, sl, :] = _dot(h, w_ref[:, E_GOG:E_AQ]).astype(BF16)
        aq = _head_norm(_dot(h, w_ref[:, E_AQ:E_AK]), bdq_ref, qn_ref)
        acc = _dot(h, w_ref[:, E_AK:E_END])
        ak = _head_norm(acc[:, :E_AV - E_AK], bdk_ref, kn_ref)
        if rope:
            aq = _rope(aq, cs_ref[sl, :], sn_ref[sl, :])
            ak = _rope(ak, cs_ref[sl, :], sn_ref[sl, :])
        aq_ref[0, sl, :] = (aq * (ATT_HEAD_DIM ** -0.5 * LOG2E)).astype(BF16)
        ak_ref[0, sl, :] = ak.astype(BF16)
        av = acc[:, E_AV - E_AK:E_GZ - E_AK]
        first = lax.broadcasted_iota(jnp.int32, av.shape, 1) < ATT_HEAD_DIM
        av_ref[0, sl, :] = jnp.concatenate([jnp.where(first, av, 1.0), jnp.where(first, 1.0, av)],
                                           axis=1).astype(BF16)
        z = acc[:, E_GZ - E_AK:].astype(BF16)
        g = _log_sigmoid(_dot(z, wg_ref[...]) + bg_ref[...]) * (LOG2E / GLA_GATE_TAU)
        gf_ref[0, sl, :] = g[:, :nk]
        gb_ref[0, sl, :] = g[:, nk:]


def _pre_even(x, mods, mod_row, consts, rope_tables, tm):
    nk = GLA_HEADS * GLA_DK
    widths = (nk, nk, GLA_HEADS * GLA_DV, GLA_HEADS * GLA_DV, nk, nk, ATT_Q_HEADS * ATT_HEAD_DIM,
              ATT_KV_HEADS * ATT_HEAD_DIM, ATT_KV_HEADS * LANES)
    dtypes = (BF16, BF16, BF16, BF16, F32, F32, BF16, BF16, BF16)
    rope = rope_tables is not None
    return _row_call(functools.partial(_pre_even_kernel, rope=rope), "pre_even_lat" if rope else "pre_even_ctx",
                     x, mods, mod_row, consts, rope_tables if rope else (), widths, dtypes, tm)


def _pre_odd_kernel(x_ref, mod_ref, gain_ref, w_ref, lb_ref, *out_refs, layer, with_q):
    if with_q:
        q_ref, kf_ref, kb_ref, gf_ref, gb_ref, v_ref, og_ref = out_refs
    else:
        kf_ref, kb_ref, gf_ref, gb_ref, v_ref = out_refs
    D = x_ref.shape[-1]
    F = v_ref.shape[-1]
    subs = _sub_tiles(x_ref.shape[1])
    hs = [_modulated_norm(x_ref[0, sl, :], mod_ref[0], gain_ref[...], 0, D).astype(BF16) for sl in subs]
    lbs = []
    for d in range(2):
        raw = lb_ref[d]
        p = jnp.exp(raw - jnp.max(raw, axis=0, keepdims=True))
        p = p / jnp.sum(p, axis=0, keepdims=True)
        lbs.append(jnp.sum(p[1:layer + 1], axis=0, keepdims=True) if layer > 0 else jnp.zeros_like(p[0:1]))
    for sl, h in zip(subs, hs):
        if with_q:
            q_ref[0, sl, :] = _silu(_dot(h, w_ref[:, 0:F])).astype(BF16)
        for d, (k_ref, g_ref) in enumerate(((kf_ref, gf_ref), (kb_ref, gb_ref))):
            lb = lbs[d]
            x = _dot(h, w_ref[:, (1 + d) * F:(2 + d) * F])
            t = jnp.exp(-jnp.abs(x))
            r = 1.0 / (1.0 + t)
            tr = t * r
            sig_pos = jnp.where(x >= 0, r, tr)
            sig_neg = jnp.where(x >= 0, tr, r)
            g_ref[0, sl, :] = jnp.log2(lb + (1.0 - lb) * sig_pos)
            k_ref[0, sl, :] = ((1.0 - lb) * sig_neg).astype(BF16)
        v_ref[0, sl, :] = _dot(h, w_ref[:, 3 * F:4 * F]).astype(BF16)
        if with_q:
            og_ref[0, sl, :] = _dot(h, w_ref[:, 4 * F:5 * F]).astype(BF16)


def _pre_odd(x, mods, mod_row, gain, w, lower_bounds, layer, tm, name, with_q=True):
    F = w.shape[1] // 5
    fields = _SEG_FIELDS if with_q else _SEG_FIELDS[1:6]
    dtype = dict(zip(_SEG_FIELDS, (BF16, BF16, BF16, F32, F32, BF16, BF16)))
    outs = _row_call(functools.partial(_pre_odd_kernel, layer=layer, with_q=with_q), name, x, mods, mod_row,
                     (gain, w, lower_bounds), (), (F,) * len(fields), [dtype[f] for f in fields], tm)
    return dict(zip(fields, outs))


def _scan_constants(reverse):
    C = CHUNK
    i = np.arange(C)[:, None]
    t = np.arange(C)[None, :]
    tri = (t >= i) if reverse else (t <= i)
    sg, pm = [], []
    size = C
    while size >= 2:
        half = size // 2
        a = (i // size) * size + half
        a_t = (t // size) * size + half
        same = (i // size) == (t // size)
        qside = (i < a) if reverse else (i >= a)
        sg.append(np.broadcast_to(np.where(qside, 1.0, -1.0), (C, LANES)))
        pm.append(same & qside & ((t >= a_t) if reverse else (t < a_t)))
        size = half
    cat = lambda xs: np.concatenate(xs, 0).astype(np.float32)
    return tri.astype(np.float32), cat(sg), cat(pm)


def _anchor_rows(G, size):
    C = G.shape[0]
    half = size // 2
    if size >= 8:
        parts = [jnp.broadcast_to(G[b * size + half:b * size + half + 1, :], (size, G.shape[1]))
                 for b in range(C // size)]
        return jnp.concatenate(parts, axis=0) if len(parts) > 1 else parts[0]
    row = lax.broadcasted_iota(jnp.int32, G.shape, 0) % size
    out = G
    for r in range(size):
        if r != half:
            out = jnp.where(row == r, pltpu.roll(G, (C - (half - r)) % C, 0), out)
    return out


def _scan_kernel(*refs, hpg, n_lat_chunks, n_ctx_chunks, ctx_out, unroll):
    def segment(seg_refs):
        with_q = len(seg_refs) == len(_SEG_FIELDS)
        rest = seg_refs[1:6] if with_q else seg_refs
        return dict(q=seg_refs[0] if with_q else None, k=rest[0:2], g=rest[2:4], v=rest[4],
                    og=seg_refs[6] if with_q else None)

    n_ctx_refs = len(_SEG_FIELDS) if ctx_out else len(_SEG_FIELDS) - 2
    lat_refs = segment(refs[:len(_SEG_FIELDS)])
    refs = refs[len(_SEG_FIELDS):]
    ctx_refs = segment(refs[:n_ctx_refs])
    gain_ref, tri_ref, sg_ref, pm_ref = refs[n_ctx_refs:n_ctx_refs + 4]
    refs = refs[n_ctx_refs + 4:]
    n_out = 2 if ctx_out else 1
    out_refs = refs[:n_out]
    of_lat, ob_lat, of_ctx, ob_ctx, st_scr = refs[n_out:]
    C = CHUNK
    K = LANES // hpg
    DV = LANES

    st_scr[...] = jnp.zeros_like(st_scr)
    lane = lax.broadcasted_iota(jnp.int32, (C, LANES), 1)
    row = lax.broadcasted_iota(jnp.int32, (C, LANES), 0)
    st_row = lax.broadcasted_iota(jnp.int32, (hpg * DV, LANES), 0)
    st_lane = lax.broadcasted_iota(jnp.int32, (hpg * DV, LANES), 1)
    st_mask = (st_row // DV) == (st_lane // K)

    def load(seg, d, row_starts):
        q_ref, v_ref, k_ref, g_ref = seg["q"], seg["v"], seg["k"][d], seg["g"][d]
        gs = [g_ref[0, pl.ds(r0, C), :] for r0 in row_starts]
        parts = []
        for g in gs:
            g_hi = g.astype(BF16)
            parts += [g_hi, (g - g_hi.astype(F32)).astype(BF16)]
        r = _dot(tri_ref[d], jnp.concatenate(parts, axis=1))
        return [dict(d=d, r0=r0, g=g, cum=r[:, 2 * i * LANES:(2 * i + 1) * LANES]
                     + r[:, (2 * i + 1) * LANES:(2 * i + 2) * LANES],
                     q=None if q_ref is None else q_ref[0, pl.ds(r0, C), :].astype(F32),
                     k=k_ref[0, pl.ds(r0, C), :].astype(F32),
                     v=v_ref[0, pl.ds(r0, C), :]) for i, (r0, g) in enumerate(zip(row_starts, gs))]

    def level_scores(it, l):
        d, cum, g = it["d"], it["cum"], it["g"]
        size = C >> l
        half = size // 2
        later_half = (row & half) != 0
        qside = jnp.logical_not(later_half) if d == 1 else later_half
        nxt = lambda n: pltpu.roll(g, C - n, 0)
        if size == 2:
            delta = jnp.where((row & 1) == 0, g if d == 1 else nxt(1), 0.0)
        elif size == 4:
            r4 = row & 3
            if d == 1:
                r0_, r1_, r3_ = g + nxt(1), g, pltpu.roll(g, 1, 0)
            else:
                r0_, r1_, r3_ = nxt(1) + nxt(2), nxt(1), g
            delta = jnp.where(r4 == 0, r0_, jnp.where(r4 == 1, r1_, jnp.where(r4 == 2, 0.0, r3_)))
        else:
            delta = (cum - _anchor_rows(cum, size)) * sg_ref[d, l * C:(l + 1) * C, :]
        t = jnp.where(qside, it["q"], it["k"]) * jnp.exp2(delta)
        pm = pm_ref[d, l * C:(l + 1) * C, :]
        if hpg == 1:
            return _dot_nt(t.astype(BF16), t.astype(BF16)) * pm
        lhs = jnp.concatenate([jnp.where(lane // K == hh, t, 0.0) for hh in range(hpg)], axis=0)
        return _dot_nt(lhs.astype(BF16), t.astype(BF16)) * jnp.concatenate([pm] * hpg, axis=0)

    def write_output(it, acc, st, o_scr):
        cum, q, k, v = it["cum"], it["q"], it["k"], it["v"]
        qk = q * k
        inter = _dot_nt((q * jnp.exp2(cum)).astype(BF16), st.astype(BF16))
        outs = []
        for hh in range(hpg):
            v_h = v[:, hh * DV:(hh + 1) * DV]
            diag = jnp.sum(qk if hpg == 1 else jnp.where(lane // K == hh, qk, 0.0), axis=1, keepdims=True)
            a_h = acc[hh * C:(hh + 1) * C].astype(BF16)
            outs.append(_dot(a_h, v_h) + diag * v_h.astype(F32) + inter[:, hh * DV:(hh + 1) * DV])
        o_scr[pl.ds(it["r0"], C), :] = jnp.concatenate(outs, axis=1) if hpg > 1 else outs[0]

    def next_state(it, st):
        d, cum = it["d"], it["cum"]
        edge = cum[0:1] if d == 1 else cum[C - 1:C]
        ks = (it["k"] * jnp.exp2(edge - cum)).astype(BF16)
        upd = _dot_tn(it["v"], ks)
        if hpg > 1:
            upd = jnp.where(st_mask, upd, 0.0)
        return st * jnp.exp2(edge) + upd

    def trip(seg, o_scrs, n_seg, t, n_u):
        start = lambda c: c * C if isinstance(c, int) else pl.multiple_of(c * C, C)
        fwd = load(seg, 0, [start(t * n_u + u) for u in range(n_u)])
        bwd = load(seg, 1, [start(n_seg - 1 - (t * n_u + u)) for u in range(n_u)])
        sts = [st_scr[0], st_scr[1]]
        for u in range(n_u):
            pair = [fwd[u], bwd[u]]
            if o_scrs is not None:
                accs = [level_scores(it, 0) for it in pair]
                for l in range(1, N_LEVELS):
                    accs = [acc + level_scores(it, l) for it, acc in zip(pair, accs)]
            for i, it in enumerate(pair):
                if o_scrs is not None:
                    write_output(it, accs[i], sts[it["d"]], o_scrs[it["d"]])
                sts[it["d"]] = next_state(it, sts[it["d"]])
        st_scr[0] = sts[0]
        st_scr[1] = sts[1]

    def run_segment(seg, o_scrs, n_seg):
        n_u = min(unroll, n_seg)
        n_trips = n_seg // n_u
        if n_trips == 1:
            trip(seg, o_scrs, n_seg, 0, n_u)
        else:
            def body(t, carry):
                trip(seg, o_scrs, n_seg, t, n_u)
                return carry
            lax.fori_loop(0, n_trips, body, 0)

    def readout_chunks(o_scrs, og_ref, out_ref, c0, n):
        for u in range(n):
            c = c0 + u
            r0 = c * C if isinstance(c, int) else pl.multiple_of(c * C, C)
            o = o_scrs[0][pl.ds(r0, C), :] + o_scrs[1][pl.ds(r0, C), :]
            gate = _silu(og_ref[0, pl.ds(r0, C), :].astype(F32))
            outs = [_rms_rows(o[:, hh * DV:(hh + 1) * DV], gain_ref[...]) * gate[:, hh * DV:(hh + 1) * DV]
                    for hh in range(hpg)]
            y = jnp.concatenate(outs, axis=1) if hpg > 1 else outs[0]
            out_ref[0, pl.ds(r0, C), :] = y.astype(out_ref.dtype)

    def readout(o_scrs, og_ref, out_ref, n_seg):
        ru = min(READOUT_UNROLL, n_seg)
        if n_seg == ru:
            readout_chunks(o_scrs, og_ref, out_ref, 0, ru)
        else:
            def body(t, carry):
                readout_chunks(o_scrs, og_ref, out_ref, t * ru, ru)
                return carry
            lax.fori_loop(0, n_seg // ru, body, 0)

    run_segment(ctx_refs, (of_ctx, ob_ctx) if ctx_out else None, n_ctx_chunks)
    run_segment(lat_refs, (of_lat, ob_lat), n_lat_chunks)
    readout((of_lat, ob_lat), lat_refs["og"], out_refs[0], n_lat_chunks)
    if ctx_out:
        readout((of_ctx, ob_ctx), ctx_refs["og"], out_refs[1], n_ctx_chunks)


def _scan(lat, ctx, gain, *, hpg, ctx_out, name):
    ctx_fields = _SEG_FIELDS if ctx_out else _SEG_FIELDS[1:6]
    lat = [lat[f] for f in _SEG_FIELDS]
    ctx = [ctx[f] for f in ctx_fields]
    B, T, HK = lat[0].shape
    n_ctx = ctx[0].shape[1]
    n_groups = HK // LANES
    wv = hpg * LANES
    unroll = SCAN_UNROLL
    for rows in (T, n_ctx):
        assert rows % CHUNK == 0 and (rows // CHUNK) % min(unroll, rows // CHUNK) == 0
    assert (T // CHUNK) % READOUT_UNROLL == 0
    tri, sg, pm = (np.stack(c) for c in zip(_scan_constants(False), _scan_constants(True)))
    tri = jnp.asarray(tri, BF16)
    sg = jnp.asarray(sg, F32)
    pm = jnp.asarray(pm, F32)

    def seq(rows, widths):
        return [pl.BlockSpec((1, rows, w_), lambda b, j: (b, 0, j)) for w_ in widths]

    width = dict(zip(_SEG_FIELDS, (LANES,) * 5 + (wv, wv)))
    out_rows = (T, n_ctx) if ctx_out else (T,)
    kern = functools.partial(_scan_kernel, hpg=hpg, n_lat_chunks=T // CHUNK, n_ctx_chunks=n_ctx // CHUNK,
                             ctx_out=ctx_out, unroll=unroll)
    return pl.pallas_call(
        kern,
        grid=(B, n_groups),
        in_specs=seq(T, [width[f] for f in _SEG_FIELDS]) + seq(n_ctx, [width[f] for f in ctx_fields])
        + [_const_spec(a.shape) for a in (gain, tri, sg, pm)],
        out_specs=[pl.BlockSpec((1, r, wv), lambda b, j: (b, 0, j)) for r in out_rows],
        out_shape=[jax.ShapeDtypeStruct((B, r, n_groups * wv), BF16) for r in out_rows],
        scratch_shapes=[pltpu.VMEM((T, wv), F32), pltpu.VMEM((T, wv), F32),
                        pltpu.VMEM((n_ctx, wv), F32), pltpu.VMEM((n_ctx, wv), F32),
                        pltpu.VMEM((2, wv, LANES), F32)],
        compiler_params=_params("arbitrary", "arbitrary"),
        name=name,
    )(*lat, *ctx, gain, tri, sg, pm)


def _attn_kernel(q_ref, *refs, n_seg):
    o_ref = refs[2 * n_seg]
    if n_seg == 1:
        keys = lambda: refs[0][0]
        vals = lambda kv: refs[1][0, :, kv * LANES:(kv + 1) * LANES]
    else:
        k_all, v_all = refs[2 * n_seg + 1:]

        @pl.when((pl.program_id(1) == 0) & (pl.program_id(2) == 0))
        def _():
            r = 0
            for s in range(n_seg):
                n = refs[2 * s].shape[1]
                k_all[r:r + n, :] = refs[2 * s][0]
                v_all[r:r + n, :] = refs[2 * s + 1][0]
                r += n

        keys = lambda: k_all[...]
        vals = lambda kv: v_all[:, kv * LANES:(kv + 1) * LANES]
    lane = lax.broadcasted_iota(jnp.int32, (q_ref.shape[1], LANES), 1)
    for grp in range(q_ref.shape[2] // LANES):
        cols = slice(grp * LANES, (grp + 1) * LANES)
        q2 = q_ref[0, :, cols]
        outs = []
        for kv in range(ATT_KV_HEADS):
            qm = jnp.where(lane // ATT_HEAD_DIM == kv, q2, jnp.zeros_like(q2))
            s = _dot_nt(qm, keys())
            p = jnp.exp2((s - jnp.max(s, axis=-1, keepdims=True)).astype(BF16))
            pv = _dot(p, vals(kv))
            outs.append(pv / pltpu.roll(pv, ATT_HEAD_DIM, 1))
        o_ref[0, :, cols] = jnp.where(lane // ATT_HEAD_DIM == 0, outs[0], outs[1]).astype(o_ref.dtype)


def _attention(aq, kvs, name):
    B, R, W = aq.shape
    tq = min(ATT_Q_TILE, R)
    assert R % tq == 0
    wq = ATT_Q_GROUPS * LANES
    in_specs = [pl.BlockSpec((1, tq, wq), lambda b, i, j: (b, i, j))]
    args = [aq]
    for k, v in kvs:
        in_specs += [pl.BlockSpec((1,) + k.shape[1:], lambda b, i, j: (b, 0, 0)),
                     pl.BlockSpec((1,) + v.shape[1:], lambda b, i, j: (b, 0, 0))]
        args += [k, v]
    n_keys = sum(k.shape[1] for k, _ in kvs)
    scratch = [] if len(kvs) == 1 else [pltpu.VMEM((n_keys, kvs[0][0].shape[2]), BF16),
                                        pltpu.VMEM((n_keys, kvs[0][1].shape[2]), BF16)]
    return pl.pallas_call(
        functools.partial(_attn_kernel, n_seg=len(kvs)),
        grid=(B, R // tq, W // wq),
        in_specs=in_specs,
        out_specs=pl.BlockSpec((1, tq, wq), lambda b, i, j: (b, i, j)),
        out_shape=jax.ShapeDtypeStruct((B, R, W), BF16),
        scratch_shapes=scratch,
        compiler_params=_params("arbitrary", "arbitrary", "arbitrary"),
        name=name,
    )(*args)


def _post_kernel(*refs, n_mix):
    x_ref, mod_ref = refs[0], refs[1]
    mix_refs = refs[2:2 + n_mix]
    wout_refs = refs[2 + n_mix:2 + 2 * n_mix]
    g_pm_ref, g_pf_ref, g_pff_ref, wg_ref, wu_ref, wd_ref, o_ref = refs[2 + 2 * n_mix:]
    D = x_ref.shape[-1]
    n_ff = wg_ref.shape[1]
    m = mod_ref[0]
    subs = _sub_tiles(x_ref.shape[1])
    ys = []
    for sl in subs:
        y = _dot(mix_refs[0][0, sl, :], wout_refs[0][...])
        for mr, wr in zip(mix_refs[1:], wout_refs[1:]):
            y = y + _dot(mr[0, sl, :], wr[...])
        ys.append(y)
    x1s = [x_ref[0, sl, :] + m[:, 2 * D:3 * D] * _rms_rows(y, g_pm_ref[...]) for sl, y in zip(subs, ys)]
    hs = [_modulated_norm(x1, m, g_pf_ref[...], 3, D).astype(BF16) for x1 in x1s]
    acts = []
    for h in hs:
        cols = []
        for c0 in range(0, n_ff, FFN_COL_TILE):
            c1 = min(c0 + FFN_COL_TILE, n_ff)
            cols.append((_silu(_dot(h, wg_ref[:, c0:c1])) * _dot(h, wu_ref[:, c0:c1])).astype(BF16))
        acts.append(jnp.concatenate(cols, axis=1))
    for sl, x1, act in zip(subs, x1s, acts):
        z = _dot(act, wd_ref[...])
        o_ref[0, sl, :] = x1 + m[:, 5 * D:6 * D] * _rms_rows(z, g_pff_ref[...])


def _post(x, mods, mod_row, mixes, consts, tm, name):
    D = x.shape[-1]
    return _row_call(functools.partial(_post_kernel, n_mix=len(mixes)), name, x, mods, mod_row, consts, mixes,
                     (D,), (F32,), tm)[0]


def _deinterleave(n_heads):
    within = np.concatenate([np.arange(0, ATT_HEAD_DIM, 2), np.arange(1, ATT_HEAD_DIM, 2)])
    return (np.arange(n_heads)[:, None] * ATT_HEAD_DIM + within[None, :]).reshape(-1)


def _attn_head_order():
    return np.array([kv * ATT_GROUP + j for j in range(ATT_GROUP) for kv in range(ATT_KV_HEADS)])


def _rope_tables(T):
    rows_n = T // GRID_W
    row = jnp.repeat(jnp.arange(rows_n), GRID_W).astype(F32)
    col = jnp.tile(jnp.arange(GRID_W), rows_n).astype(F32)
    axis_dim = ATT_HEAD_DIM // 2
    inv = ROPE_THETA ** (-jnp.arange(0, axis_dim, 2, dtype=F32) / axis_dim)
    ang = jnp.concatenate([row[:, None] * inv, col[:, None] * inv], axis=-1)
    cos, sin = jnp.cos(ang), jnp.sin(ang)
    reps = LANES // ATT_HEAD_DIM
    cs = jnp.tile(jnp.concatenate([cos, cos], axis=1), (1, reps))
    sn = jnp.tile(jnp.concatenate([-sin, sin], axis=1), (1, reps))
    return cs, sn


def _block_mean_matrix(width, group):
    idx = np.arange(width) // group
    return jnp.asarray((idx[:, None] == idx[None, :]).astype(np.float32) / group, BF16)


def kernel(x, c, ctx, c_ctx, mod_w, mod_b, norm_pre_mix, norm_post_mix, norm_pre_ffn, norm_post_ffn, even_w_in,
           gla_w_gate, gla_b_gate, gla_out_norm, att_q_norm, att_k_norm, even_w_out, odd_w_in, hgrn_lower_bounds,
           hgrn_out_norm, odd_w_out, ffn_w_gate, ffn_w_up, ffn_w_down):
    B, T, D = x.shape
    n_ctx = ctx.shape[1]
    depth = mod_w.shape[0]
    tm_lat = LAT_ROW_TILE
    tm_ctx = min(CTX_ROW_TILE, n_ctx)
    tm_post = min(POST_ROW_TILE, T)
    assert T % tm_lat == 0 and T % tm_post == 0 and n_ctx % tm_ctx == 0
    assert T % GRID_W == 0 and D % LANES == 0

    n_rows = -(-(B + 1) // 8) * 8
    cond = jnp.concatenate([c, c_ctx[None, :], jnp.zeros((n_rows - B - 1, D), F32)], axis=0)
    mods = _modulation(cond, mod_w, mod_b).reshape(depth * n_rows, 1, 6 * D)

    vec = lambda a: a.reshape(1, -1).astype(F32)
    bf = lambda a: a.astype(BF16)
    rope_tables = _rope_tables(T)
    ffn_gate16, ffn_up16, ffn_down16 = bf(ffn_w_gate), bf(ffn_w_up), bf(ffn_w_down)
    x_lat, x_ctx = x, ctx

    for l in range(depth):
        last = l == depth - 1
        j = l // 2
        lat_row = lambda b, base=l * n_rows: base + b
        ctx_row = lambda b, base=l * n_rows: base + B
        if l % 2 == 0:
            sizes = np.cumsum([0, GLA_HEADS * GLA_DK, GLA_HEADS * GLA_DK, GLA_HEADS * GLA_DV, GLA_HEADS * GLA_DV,
                               2 * GLA_GATE_RANK, ATT_Q_HEADS * ATT_HEAD_DIM, ATT_KV_HEADS * ATT_HEAD_DIM,
                               ATT_KV_HEADS * ATT_HEAD_DIM])
            o_gq, o_gk, o_gv, o_gog, o_gz, o_aq, o_ak, o_av = sizes[:-1]
            q_cols = o_aq + (_attn_head_order()[:, None] * ATT_HEAD_DIM
                             + _deinterleave(1)[None, :]).reshape(-1)
            k_cols = o_ak + _deinterleave(ATT_KV_HEADS)
            cols = np.concatenate([np.arange(o_gq, o_gz), q_cols, k_cols, np.arange(o_av, sizes[-1]),
                                   np.arange(o_gz, o_aq)])
            w_in = bf(even_w_in[j][:, cols])
            wgate = gla_w_gate[j]
            nk = GLA_HEADS * GLA_DK
            zeros = jnp.zeros((GLA_GATE_RANK, nk), F32)
            wg2 = bf(jnp.concatenate([jnp.concatenate([wgate[0], zeros], axis=1),
                                      jnp.concatenate([zeros, wgate[1]], axis=1)], axis=0))
            bg2 = gla_b_gate[j].reshape(1, 2 * nk).astype(F32)
            qn = vec(jnp.tile(att_q_norm[j][_deinterleave(1)], ATT_Q_HEADS))
            kn = vec(jnp.tile(att_k_norm[j][_deinterleave(1)], ATT_KV_HEADS))
            bdq = _block_mean_matrix(ATT_Q_HEADS * ATT_HEAD_DIM, ATT_HEAD_DIM)
            bdk = _block_mean_matrix(ATT_KV_HEADS * ATT_HEAD_DIM, ATT_HEAD_DIM)
            consts = (vec(norm_pre_mix[l]), w_in, wg2, bg2, qn, kn, bdq, bdk)
            lat = _pre_even(x_lat, mods, lat_row, consts, rope_tables, tm_lat)
            cx = _pre_even(x_ctx, mods, ctx_row, consts, None, tm_ctx)
            pick = lambda o: dict(zip(_SEG_FIELDS, (o[0], o[1], o[1], o[4], o[5], o[2], o[3])))
            mix_a = _scan(pick(lat), pick(cx), vec(gla_out_norm[j]), hpg=LANES // GLA_DK, ctx_out=not last,
                          name="scan_gla")
            mix_b_lat = _attention(lat[6], [(cx[7], cx[8]), (lat[7], lat[8])], "attention_lat")
            mixes_lat = (mix_a[0], mix_b_lat)
            if not last:
                mixes_ctx = (mix_a[1], _attention(cx[6], [(cx[7], cx[8])], "attention_ctx"))
            n_a = GLA_HEADS * GLA_DV
            b_rows = n_a + (_attn_head_order()[:, None] * ATT_HEAD_DIM
                            + np.arange(ATT_HEAD_DIM)[None, :]).reshape(-1)
            wouts = (bf(even_w_out[j][:n_a]), bf(even_w_out[j][b_rows]))
        else:
            w_in = bf(odd_w_in[j])
            lbs = hgrn_lower_bounds.astype(F32)
            lat = _pre_odd(x_lat, mods, lat_row, vec(norm_pre_mix[l]), w_in, lbs, l, tm_lat, "pre_odd_lat")
            cx = _pre_odd(x_ctx, mods, ctx_row, vec(norm_pre_mix[l]), w_in, lbs, l, tm_ctx, "pre_odd_ctx",
                          with_q=not last)
            mix = _scan(lat, cx, vec(hgrn_out_norm[j]), hpg=LANES // HGRN_DF, ctx_out=not last, name="scan_hgrn")
            mixes_lat = (mix[0],)
            if not last:
                mixes_ctx = (mix[1],)
            wouts = (bf(odd_w_out[j]),)
        consts = (*wouts, vec(norm_post_mix[l]), vec(norm_pre_ffn[l]), vec(norm_post_ffn[l]),
                  (ffn_gate16, l), (ffn_up16, l), (ffn_down16, l))
        if not last:
            x_ctx = _post(x_ctx, mods, ctx_row, mixes_ctx, consts, tm_ctx, "post_ctx")
        x_lat = _post(x_lat, mods, lat_row, mixes_lat, consts, tm_post, "post_lat")
    return x_lat
```

```python
import functools

import numpy as np
import jax
import jax.numpy as jnp
from jax import lax
from jax.experimental import pallas as pl
from jax.experimental.pallas import tpu as pltpu

GRID_W = 64
GLA_HEADS = 4
GLA_DK = 64
GLA_DV = 128
GLA_GATE_RANK = 16
GLA_GATE_TAU = 16.0
ATT_Q_HEADS = 8
ATT_KV_HEADS = 2
ATT_HEAD_DIM = 64
ATT_GROUP = ATT_Q_HEADS // ATT_KV_HEADS
ROPE_THETA = 10000.0
HGRN_HEADS = 8
HGRN_DF = 128
EPS = 1e-6

LANES = 128
VMEM_LIMIT_BYTES = 60000 * 1024

LAT_ROW_TILE = 512
POST_ROW_TILE = 1024
CTX_ROW_TILE = 256
SUB_ROWS = 256
MOD_COL_TILE = 2048
FFN_COL_TILE = 256
ATT_Q_TILE = 256
ATT_Q_GROUPS = 4
CHUNK = 128
N_LEVELS = CHUNK.bit_length() - 1
SCAN_UNROLL = 8
READOUT_UNROLL = 4

F32 = jnp.float32
BF16 = jnp.bfloat16
LOG2E = 1.4426950408889634

_SEG_FIELDS = ("q", "k_fwd", "k_bwd", "g_fwd", "g_bwd", "v", "out_gate")


def _dot(a, b):
    return jnp.dot(a, b, preferred_element_type=F32)


def _dot_nt(a, b):
    return lax.dot_general(a, b, (((1,), (1,)), ((), ())), preferred_element_type=F32)


def _dot_tn(a, b):
    return lax.dot_general(a, b, (((0,), (0,)), ((), ())), preferred_element_type=F32)


def _rms_rows(x, gain):
    return x * lax.rsqrt(jnp.mean(x * x, axis=-1, keepdims=True) + EPS) * gain


def _silu(x):
    return x / (1.0 + jnp.exp(-x))


def _log_sigmoid(x):
    return jnp.minimum(x, 0.0) - jnp.log1p(jnp.exp(-jnp.abs(x)))


def _params(*sem):
    return pltpu.CompilerParams(dimension_semantics=sem, vmem_limit_bytes=VMEM_LIMIT_BYTES)


def _const_spec(shape, layer=None):
    if layer is None:
        return pl.BlockSpec(shape, lambda *_: (0,) * len(shape), pipeline_mode=pl.Buffered(1))
    return pl.BlockSpec((None,) + tuple(shape[1:]), lambda *_: (layer,) + (0,) * (len(shape) - 1),
                        pipeline_mode=pl.Buffered(1))


def _sub_tiles(n_rows):
    sub = min(SUB_ROWS, n_rows)
    return [slice(s, s + sub) for s in range(0, n_rows, sub)]


def _mod_kernel(s_ref, w_ref, b_ref, o_ref):
    s = _silu(s_ref[...]).astype(BF16)
    o_ref[0] = _dot(s, w_ref[0].astype(BF16)) + b_ref[0]


def _modulation(cond, mod_w, mod_b):
    L, D, D6 = mod_w.shape
    R = cond.shape[0]
    tn = MOD_COL_TILE
    assert D6 % tn == 0
    return pl.pallas_call(
        _mod_kernel,
        grid=(L, D6 // tn),
        in_specs=[
            pl.BlockSpec((R, D), lambda l, j: (0, 0)),
            pl.BlockSpec((1, D, tn), lambda l, j: (l, 0, j)),
            pl.BlockSpec((1, 1, tn), lambda l, j: (l, 0, j)),
        ],
        out_specs=pl.BlockSpec((1, R, tn), lambda l, j: (l, 0, j)),
        out_shape=jax.ShapeDtypeStruct((L, R, D6), F32),
        compiler_params=_params("arbitrary", "arbitrary"),
        name="modulation",
    )(cond, mod_w, mod_b.reshape(L, 1, D6))


def _modulated_norm(x, mod_row, gain, shift_idx, D):
    shift = mod_row[:, shift_idx * D:(shift_idx + 1) * D]
    scale = mod_row[:, (shift_idx + 1) * D:(shift_idx + 2) * D]
    return _rms_rows(x, gain) * (1.0 + scale) + shift


def _row_call(kernel, name, x, mods, mod_row, consts, row_inputs, out_widths, out_dtypes, tm):
    B, R, D = x.shape
    assert R % tm == 0
    row = lambda w_: pl.BlockSpec((1, tm, w_), lambda b, i: (b, i, 0))
    in_specs = [row(D), pl.BlockSpec((1, 1, mods.shape[-1]), lambda b, i: (mod_row(b), 0, 0))]
    in_specs += [pl.BlockSpec((tm, a.shape[-1]), lambda b, i: (i, 0)) if a.ndim == 2 else row(a.shape[-1])
                 for a in row_inputs]
    in_specs += [_const_spec(a[0].shape, a[1]) if isinstance(a, tuple) else _const_spec(a.shape) for a in consts]
    consts = [a[0] if isinstance(a, tuple) else a for a in consts]
    return pl.pallas_call(
        kernel,
        grid=(B, R // tm),
        in_specs=in_specs,
        out_specs=[row(w_) for w_ in out_widths],
        out_shape=[jax.ShapeDtypeStruct((B, R, w_), dt) for w_, dt in zip(out_widths, out_dtypes)],
        compiler_params=_params("arbitrary", "arbitrary"),
        name=name,
    )(x, mods, *row_inputs, *consts)


_EVEN_WIDTHS = (GLA_HEADS * GLA_DK, GLA_HEADS * GLA_DK, GLA_HEADS * GLA_DV, GLA_HEADS * GLA_DV,
                ATT_Q_HEADS * ATT_HEAD_DIM, ATT_KV_HEADS * ATT_HEAD_DIM, ATT_KV_HEADS * ATT_HEAD_DIM,
                2 * GLA_GATE_RANK)
E_GQ, E_GK, E_GV, E_GOG, E_AQ, E_AK, E_AV, E_GZ, E_END = (int(c) for c in np.cumsum((0,) + _EVEN_WIDTHS))


def _swap_halves(x):
    w = x.shape[-1]
    half = ATT_HEAD_DIM // 2
    lane = lax.broadcasted_iota(jnp.int32, x.shape, x.ndim - 1)
    first_half = (lane % ATT_HEAD_DIM) < half
    return jnp.where(first_half, pltpu.roll(x, w - half, x.ndim - 1), pltpu.roll(x, half, x.ndim - 1))


def _head_norm(y, bd_ref, gain_ref):
    ms = _dot((y * y).astype(BF16), bd_ref[...])
    return y * lax.rsqrt(ms + EPS) * gain_ref[...]


def _rope(y, cs, sn):
    reps = y.shape[-1] // LANES
    cs = jnp.concatenate([cs] * reps, axis=1) if reps > 1 else cs
    sn = jnp.concatenate([sn] * reps, axis=1) if reps > 1 else sn
    return y * cs + _swap_halves(y) * sn


def _pre_even_kernel(*refs, rope):
    x_ref, mod_ref = refs[:2]
    refs = refs[2:]
    if rope:
        cs_ref, sn_ref = refs[:2]
        refs = refs[2:]
    gain_ref, w_ref, wg_ref, bg_ref, qn_ref, kn_ref, bdq_ref, bdk_ref = refs[:8]
    gq_ref, gk_ref, gv_ref, gog_ref, gf_ref, gb_ref, aq_ref, ak_ref, av_ref = refs[8:]
    D = x_ref.shape[-1]
    nk = GLA_HEADS * GLA_DK
    subs = _sub_tiles(x_ref.shape[1])
    for sl in subs:
        h = _modulated_norm(x_ref[0, sl, :], mod_ref[0], gain_ref[...], 0, D).astype(BF16)
        acc = _dot(h, w_ref[:, E_GQ:E_GV])
        gq_ref[0, sl, :] = (acc[:, :nk] * (GLA_DK ** -0.5)).astype(BF16)
        gk_ref[0, sl, :] = acc[:, nk:].astype(BF16)
        gv_ref[0, sl, :] = _dot(h, w_ref[:, E_GV:E_GOG]).astype(BF16)
        gog_ref[0, sl, :] = _dot(h, w_ref[:, E_GOG:E_AQ]).astype(BF16)
        aq = _head_norm(_dot(h, w_ref[:, E_AQ:E_AK]), bdq_ref, qn_ref)
        acc = _dot(h, w_ref[:, E_AK:E_END])
        ak = _head_norm(acc[:, :E_AV - E_AK], bdk_ref, kn_ref)
        if rope:
            aq = _rope(aq, cs_ref[sl, :], sn_ref[sl, :])
            ak = _rope(ak, cs_ref[sl, :], sn_ref[sl, :])
        aq_ref[0, sl, :] = (aq * (ATT_HEAD_DIM ** -0.5 * LOG2E)).astype(BF16)
        ak_ref[0, sl, :] = ak.astype(BF16)
        av = acc[:, E_AV - E_AK:E_GZ - E_AK]
        first = lax.broadcasted_iota(jnp.int32, av.shape, 1) < ATT_HEAD_DIM
        av_ref[0, sl, :] = jnp.concatenate([jnp.where(first, av, 1.0), jnp.where(first, 1.0, av)],
                                           axis=1).astype(BF16)
        z = acc[:, E_GZ - E_AK:].astype(BF16)
        g = _log_sigmoid(_dot(z, wg_ref[...]) + bg_ref[...]) * (LOG2E / GLA_GATE_TAU)
        gf_ref[0, sl, :] = g[:, :nk]
        gb_ref[0, sl, :] = g[:, nk:]


def _pre_even(x, mods, mod_row, consts, rope_tables, tm):
    nk = GLA_HEADS * GLA_DK
    widths = (nk, nk, GLA_HEADS * GLA_DV, GLA_HEADS * GLA_DV, nk, nk, ATT_Q_HEADS * ATT_HEAD_DIM,
              ATT_KV_HEADS * ATT_HEAD_DIM, ATT_KV_HEADS * LANES)
    dtypes = (BF16, BF16, BF16, BF16, F32, F32, BF16, BF16, BF16)
    rope = rope_tables is not None
    return _row_call(functools.partial(_pre_even_kernel, rope=rope), "pre_even_lat" if rope else "pre_even_ctx",
                     x, mods, mod_row, consts, rope_tables if rope else (), widths, dtypes, tm)


def _pre_odd_kernel(x_ref, mod_ref, gain_ref, w_ref, lb_ref, *out_refs, layer, with_q):
    if with_q:
        q_ref, kf_ref, kb_ref, gf_ref, gb_ref, v_ref, og_ref = out_refs
    else:
        kf_ref, kb_ref, gf_ref, gb_ref, v_ref = out_refs
    D = x_ref.shape[-1]
    F = v_ref.shape[-1]
    subs = _sub_tiles(x_ref.shape[1])
    lbs = []
    for d in range(2):
        raw = lb_ref[d]
        p = jnp.exp(raw - jnp.max(raw, axis=0, keepdims=True))
        p = p / jnp.sum(p, axis=0, keepdims=True)
        lbs.append(jnp.sum(p[1:layer + 1], axis=0, keepdims=True) if layer > 0 else jnp.zeros_like(p[0:1]))
    for sl in subs:
        h = _modulated_norm(x_ref[0, sl, :], mod_ref[0], gain_ref[...], 0, D).astype(BF16)
        if with_q:
            q_ref[0, sl, :] = _silu(_dot(h, w_ref[:, 0:F])).astype(BF16)
        for d, (k_ref, g_ref) in enumerate(((kf_ref, gf_ref), (kb_ref, gb_ref))):
            lb = lbs[d]
            x = _dot(h, w_ref[:, (1 + d) * F:(2 + d) * F])
            t = jnp.exp(-jnp.abs(x))
            r = 1.0 / (1.0 + t)
            tr = t * r
            sig_pos = jnp.where(x >= 0, r, tr)
            sig_neg = jnp.where(x >= 0, tr, r)
            g_ref[0, sl, :] = jnp.log2(lb + (1.0 - lb) * sig_pos)
            k_ref[0, sl, :] = ((1.0 - lb) * sig_neg).astype(BF16)
        v_ref[0, sl, :] = _dot(h, w_ref[:, 3 * F:4 * F]).astype(BF16)
        if with_q:
            og_ref[0, sl, :] = _dot(h, w_ref[:, 4 * F:5 * F]).astype(BF16)


def _pre_odd(x, mods, mod_row, gain, w, lower_bounds, layer, tm, name, with_q=True):
    F = w.shape[1] // 5
    fields = _SEG_FIELDS if with_q else _SEG_FIELDS[1:6]
    dtype = dict(zip(_SEG_FIELDS, (BF16, BF16, BF16, F32, F32, BF16, BF16)))
    outs = _row_call(functools.partial(_pre_odd_kernel, layer=layer, with_q=with_q), name, x, mods, mod_row,
                     (gain, w, lower_bounds), (), (F,) * len(fields), [dtype[f] for f in fields], tm)
    return dict(zip(fields, outs))


def _scan_constants(reverse):
    C = CHUNK
    i = np.arange(C)[:, None]
    t = np.arange(C)[None, :]
    tri = (t >= i) if reverse else (t <= i)
    sg, pm = [], []
    size = C
    while size >= 2:
        half = size // 2
        a = (i // size) * size + half
        a_t = (t // size) * size + half
        same = (i // size) == (t // size)
        qside = (i < a) if reverse else (i >= a)
        sg.append(np.broadcast_to(np.where(qside, 1.0, -1.0), (C, LANES)))
        pm.append(same & qside & ((t >= a_t) if reverse else (t < a_t)))
        size = half
    cat = lambda xs: np.concatenate(xs, 0).astype(np.float32)
    return tri.astype(np.float32), cat(sg), cat(pm)


def _anchor_rows(G, size):
    C = G.shape[0]
    half = size // 2
    if size >= 8:
        parts = [jnp.broadcast_to(G[b * size + half:b * size + half + 1, :], (size, G.shape[1]))
                 for b in range(C // size)]
        return jnp.concatenate(parts, axis=0) if len(parts) > 1 else parts[0]
    row = lax.broadcasted_iota(jnp.int32, G.shape, 0) % size
    out = G
    for r in range(size):
        if r != half:
            out = jnp.where(row == r, pltpu.roll(G, (C - (half - r)) % C, 0), out)
    return out


def _scan_kernel(*refs, hpg, n_lat_chunks, n_ctx_chunks, ctx_out, unroll):
    def segment(seg_refs):
        with_q = len(seg_refs) == len(_SEG_FIELDS)
        rest = seg_refs[1:6] if with_q else seg_refs
        return dict(q=seg_refs[0] if with_q else None, k=rest[0:2], g=rest[2:4], v=rest[4],
                    og=seg_refs[6] if with_q else None)

    n_ctx_refs = len(_SEG_FIELDS) if ctx_out else len(_SEG_FIELDS) - 2
    lat_refs = segment(refs[:len(_SEG_FIELDS)])
    refs = refs[len(_SEG_FIELDS):]
    ctx_refs = segment(refs[:n_ctx_refs])
    gain_ref, tri_ref, sg_ref, pm_ref = refs[n_ctx_refs:n_ctx_refs + 4]
    refs = refs[n_ctx_refs + 4:]
    n_out = 2 if ctx_out else 1
    out_refs = refs[:n_out]
    of_lat, ob_lat, of_ctx, ob_ctx, st_scr = refs[n_out:]
    C = CHUNK
    K = LANES // hpg
    DV = LANES

    st_scr[...] = jnp.zeros_like(st_scr)
    lane = lax.broadcasted_iota(jnp.int32, (C, LANES), 1)
    row = lax.broadcasted_iota(jnp.int32, (C, LANES), 0)
    st_row = lax.broadcasted_iota(jnp.int32, (hpg * DV, LANES), 0)
    st_lane = lax.broadcasted_iota(jnp.int32, (hpg * DV, LANES), 1)
    st_mask = (st_row // DV) == (st_lane // K)

    def load(seg, d, row_starts):
        q_ref, v_ref, k_ref, g_ref = seg["q"], seg["v"], seg["k"][d], seg["g"][d]
        gs = [g_ref[0, pl.ds(r0, C), :] for r0 in row_starts]
        parts = []
        for g in gs:
            g_hi = g.astype(BF16)
            parts += [g_hi, (g - g_hi.astype(F32)).astype(BF16)]
        r = _dot(tri_ref[d], jnp.concatenate(parts, axis=1))
        return [dict(d=d, r0=r0, g=g, cum=r[:, 2 * i * LANES:(2 * i + 1) * LANES]
                     + r[:, (2 * i + 1) * LANES:(2 * i + 2) * LANES],
                     q=None if q_ref is None else q_ref[0, pl.ds(r0, C), :].astype(F32),
                     k=k_ref[0, pl.ds(r0, C), :].astype(F32),
                     v=v_ref[0, pl.ds(r0, C), :]) for i, (r0, g) in enumerate(zip(row_starts, gs))]

    def level_scores(it, l):
        d, cum, g = it["d"], it["cum"], it["g"]
        size = C >> l
        half = size // 2
        later_half = (row & half) != 0
        qside = jnp.logical_not(later_half) if d == 1 else later_half
        nxt = lambda n: pltpu.roll(g, C - n, 0)
        if size == 2:
            delta = jnp.where((row & 1) == 0, g if d == 1 else nxt(1), 0.0)
        elif size == 4:
            r4 = row & 3
            if d == 1:
                r0_, r1_, r3_ = g + nxt(1), g, pltpu.roll(g, 1, 0)
            else:
                r0_, r1_, r3_ = nxt(1) + nxt(2), nxt(1), g
            delta = jnp.where(r4 == 0, r0_, jnp.where(r4 == 1, r1_, jnp.where(r4 == 2, 0.0, r3_)))
        else:
            delta = (cum - _anchor_rows(cum, size)) * sg_ref[d, l * C:(l + 1) * C, :]
        t = jnp.where(qside, it["q"], it["k"]) * jnp.exp2(delta)
        pm = pm_ref[d, l * C:(l + 1) * C, :]
        if hpg == 1:
            return _dot_nt(t.astype(BF16), t.astype(BF16)) * pm
        lhs = jnp.concatenate([jnp.where(lane // K == hh, t, 0.0) for hh in range(hpg)], axis=0)
        return _dot_nt(lhs.astype(BF16), t.astype(BF16)) * jnp.concatenate([pm] * hpg, axis=0)

    def write_output(it, acc, st, o_scr):
        cum, q, k, v = it["cum"], it["q"], it["k"], it["v"]
        qk = q * k
        inter = _dot_nt((q * jnp.exp2(cum)).astype(BF16), st.astype(BF16))
        outs = []
        for hh in range(hpg):
            v_h = v[:, hh * DV:(hh + 1) * DV]
            diag = jnp.sum(qk if hpg == 1 else jnp.where(lane // K == hh, qk, 0.0), axis=1, keepdims=True)
            a_h = acc[hh * C:(hh + 1) * C].astype(BF16)
            outs.append(_dot(a_h, v_h) + diag * v_h.astype(F32) + inter[:, hh * DV:(hh + 1) * DV])
        o_scr[pl.ds(it["r0"], C), :] = jnp.concatenate(outs, axis=1) if hpg > 1 else outs[0]

    def next_state(it, st):
        d, cum = it["d"], it["cum"]
        edge = cum[0:1] if d == 1 else cum[C - 1:C]
        ks = (it["k"] * jnp.exp2(edge - cum)).astype(BF16)
        upd = _dot_tn(it["v"], ks)
        if hpg > 1:
            upd = jnp.where(st_mask, upd, 0.0)
        return st * jnp.exp2(edge) + upd

    def trip(seg, o_scrs, n_seg, t, n_u):
        start = lambda c: c * C if isinstance(c, int) else pl.multiple_of(c * C, C)
        fwd = load(seg, 0, [start(t * n_u + u) for u in range(n_u)])
        bwd = load(seg, 1, [start(n_seg - 1 - (t * n_u + u)) for u in range(n_u)])
        sts = [st_scr[0], st_scr[1]]
        for u in range(n_u):
            pair = [fwd[u], bwd[u]]
            if o_scrs is not None:
                accs = [level_scores(it, 0) for it in pair]
                for l in range(1, N_LEVELS):
                    accs = [acc + level_scores(it, l) for it, acc in zip(pair, accs)]
            for i, it in enumerate(pair):
                if o_scrs is not None:
                    write_output(it, accs[i], sts[it["d"]], o_scrs[it["d"]])
                sts[it["d"]] = next_state(it, sts[it["d"]])
        st_scr[0] = sts[0]
        st_scr[1] = sts[1]

    def run_segment(seg, o_scrs, n_seg):
        n_u = min(unroll, n_seg)
        n_trips = n_seg // n_u
        if n_trips == 1:
            trip(seg, o_scrs, n_seg, 0, n_u)
        else:
            def body(t, carry):
                trip(seg, o_scrs, n_seg, t, n_u)
                return carry
            lax.fori_loop(0, n_trips, body, 0)

    def readout_chunks(o_scrs, og_ref, out_ref, c0, n):
        for u in range(n):
            c = c0 + u
            r0 = c * C if isinstance(c, int) else pl.multiple_of(c * C, C)
            o = o_scrs[0][pl.ds(r0, C), :] + o_scrs[1][pl.ds(r0, C), :]
            gate = _silu(og_ref[0, pl.ds(r0, C), :].astype(F32))
            outs = [_rms_rows(o[:, hh * DV:(hh + 1) * DV], gain_ref[...]) * gate[:, hh * DV:(hh + 1) * DV]
                    for hh in range(hpg)]
            y = jnp.concatenate(outs, axis=1) if hpg > 1 else outs[0]
            out_ref[0, pl.ds(r0, C), :] = y.astype(out_ref.dtype)

    def readout(o_scrs, og_ref, out_ref, n_seg):
        ru = min(READOUT_UNROLL, n_seg)
        if n_seg == ru:
            readout_chunks(o_scrs, og_ref, out_ref, 0, ru)
        else:
            def body(t, carry):
                readout_chunks(o_scrs, og_ref, out_ref, t * ru, ru)
                return carry
            lax.fori_loop(0, n_seg // ru, body, 0)

    run_segment(ctx_refs, (of_ctx, ob_ctx) if ctx_out else None, n_ctx_chunks)
    run_segment(lat_refs, (of_lat, ob_lat), n_lat_chunks)
    readout((of_lat, ob_lat), lat_refs["og"], out_refs[0], n_lat_chunks)
    if ctx_out:
        readout((of_ctx, ob_ctx), ctx_refs["og"], out_refs[1], n_ctx_chunks)


def _scan(lat, ctx, gain, *, hpg, ctx_out, name):
    ctx_fields = _SEG_FIELDS if ctx_out else _SEG_FIELDS[1:6]
    lat = [lat[f] for f in _SEG_FIELDS]
    ctx = [ctx[f] for f in ctx_fields]
    B, T, HK = lat[0].shape
    n_ctx = ctx[0].shape[1]
    n_groups = HK // LANES
    wv = hpg * LANES
    unroll = SCAN_UNROLL
    for rows in (T, n_ctx):
        assert rows % CHUNK == 0 and (rows // CHUNK) % min(unroll, rows // CHUNK) == 0
    assert (T // CHUNK) % READOUT_UNROLL == 0
    tri, sg, pm = (np.stack(c) for c in zip(_scan_constants(False), _scan_constants(True)))
    tri = jnp.asarray(tri, BF16)
    sg = jnp.asarray(sg, F32)
    pm = jnp.asarray(pm, F32)

    def seq(rows, widths):
        return [pl.BlockSpec((1, rows, w_), lambda b, j: (b, 0, j)) for w_ in widths]

    width = dict(zip(_SEG_FIELDS, (LANES,) * 5 + (wv, wv)))
    out_rows = (T, n_ctx) if ctx_out else (T,)
    kern = functools.partial(_scan_kernel, hpg=hpg, n_lat_chunks=T // CHUNK, n_ctx_chunks=n_ctx // CHUNK,
                             ctx_out=ctx_out, unroll=unroll)
    return pl.pallas_call(
        kern,
        grid=(B, n_groups),
        in_specs=seq(T, [width[f] for f in _SEG_FIELDS]) + seq(n_ctx, [width[f] for f in ctx_fields])
        + [_const_spec(a.shape) for a in (gain, tri, sg, pm)],
        out_specs=[pl.BlockSpec((1, r, wv), lambda b, j: (b, 0, j)) for r in out_rows],
        out_shape=[jax.ShapeDtypeStruct((B, r, n_groups * wv), BF16) for r in out_rows],
        scratch_shapes=[pltpu.VMEM((T, wv), F32), pltpu.VMEM((T, wv), F32),
                        pltpu.VMEM((n_ctx, wv), F32), pltpu.VMEM((n_ctx, wv), F32),
                        pltpu.VMEM((2, wv, LANES), F32)],
        compiler_params=_params("arbitrary", "arbitrary"),
        name=name,
    )(*lat, *ctx, gain, tri, sg, pm)


def _attn_kernel(q_ref, *refs, n_seg):
    o_ref = refs[2 * n_seg]
    if n_seg == 1:
        keys = lambda: refs[0][0]
        vals = lambda kv: refs[1][0, :, kv * LANES:(kv + 1) * LANES]
    else:
        k_all, v_all = refs[2 * n_seg + 1:]

        @pl.when((pl.program_id(1) == 0) & (pl.program_id(2) == 0))
        def _():
            r = 0
            for s in range(n_seg):
                n = refs[2 * s].shape[1]
                k_all[r:r + n, :] = refs[2 * s][0]
                v_all[r:r + n, :] = refs[2 * s + 1][0]
                r += n

        keys = lambda: k_all[...]
        vals = lambda kv: v_all[:, kv * LANES:(kv + 1) * LANES]
    lane = lax.broadcasted_iota(jnp.int32, (q_ref.shape[1], LANES), 1)
    for grp in range(q_ref.shape[2] // LANES):
        cols = slice(grp * LANES, (grp + 1) * LANES)
        q2 = q_ref[0, :, cols]
        outs = []
        for kv in range(ATT_KV_HEADS):
            qm = jnp.where(lane // ATT_HEAD_DIM == kv, q2, jnp.zeros_like(q2))
            s = _dot_nt(qm, keys())
            p = jnp.exp2((s - jnp.max(s, axis=-1, keepdims=True)).astype(BF16))
            pv = _dot(p, vals(kv))
            outs.append(pv / pltpu.roll(pv, ATT_HEAD_DIM, 1))
        o_ref[0, :, cols] = jnp.where(lane // ATT_HEAD_DIM == 0, outs[0], outs[1]).astype(o_ref.dtype)


def _attention(aq, kvs, name):
    B, R, W = aq.shape
    tq = min(ATT_Q_TILE, R)
    assert R % tq == 0
    wq = ATT_Q_GROUPS * LANES
    in_specs = [pl.BlockSpec((1, tq, wq), lambda b, i, j: (b, i, j))]
    args = [aq]
    for k, v in kvs:
        in_specs += [pl.BlockSpec((1,) + k.shape[1:], lambda b, i, j: (b, 0, 0)),
                     pl.BlockSpec((1,) + v.shape[1:], lambda b, i, j: (b, 0, 0))]
        args += [k, v]
    n_keys = sum(k.shape[1] for k, _ in kvs)
    scratch = [] if len(kvs) == 1 else [pltpu.VMEM((n_keys, kvs[0][0].shape[2]), BF16),
                                        pltpu.VMEM((n_keys, kvs[0][1].shape[2]), BF16)]
    return pl.pallas_call(
        functools.partial(_attn_kernel, n_seg=len(kvs)),
        grid=(B, R // tq, W // wq),
        in_specs=in_specs,
        out_specs=pl.BlockSpec((1, tq, wq), lambda b, i, j: (b, i, j)),
        out_shape=jax.ShapeDtypeStruct((B, R, W), BF16),
        scratch_shapes=scratch,
        compiler_params=_params("arbitrary", "arbitrary", "arbitrary"),
        name=name,
    )(*args)


def _post_kernel(*refs, n_mix):
    x_ref, mod_ref = refs[0], refs[1]
    mix_refs = refs[2:2 + n_mix]
    wout_refs = refs[2 + n_mix:2 + 2 * n_mix]
    g_pm_ref, g_pf_ref, g_pff_ref, wg_ref, wu_ref, wd_ref, o_ref = refs[2 + 2 * n_mix:]
    D = x_ref.shape[-1]
    n_ff = wg_ref.shape[1]
    m = mod_ref[0]
    subs = _sub_tiles(x_ref.shape[1])
    ys = []
    for sl in subs:
        y = _dot(mix_refs[0][0, sl, :], wout_refs[0][...])
        for mr, wr in zip(mix_refs[1:], wout_refs[1:]):
            y = y + _dot(mr[0, sl, :], wr[...])
        ys.append(y)
    x1s = [x_ref[0, sl, :] + m[:, 2 * D:3 * D] * _rms_rows(y, g_pm_ref[...]) for sl, y in zip(subs, ys)]
    hs = [_modulated_norm(x1, m, g_pf_ref[...], 3, D).astype(BF16) for x1 in x1s]
    acts = []
    for h in hs:
        cols = []
        for c0 in range(0, n_ff, FFN_COL_TILE):
            c1 = min(c0 + FFN_COL_TILE, n_ff)
            cols.append((_silu(_dot(h, wg_ref[:, c0:c1])) * _dot(h, wu_ref[:, c0:c1])).astype(BF16))
        acts.append(jnp.concatenate(cols, axis=1))
    for sl, x1, act in zip(subs, x1s, acts):
        z = _dot(act, wd_ref[...])
        o_ref[0, sl, :] = x1 + m[:, 5 * D:6 * D] * _rms_rows(z, g_pff_ref[...])


def _post(x, mods, mod_row, mixes, consts, tm, name):
    D = x.shape[-1]
    return _row_call(functools.partial(_post_kernel, n_mix=len(mixes)), name, x, mods, mod_row, consts, mixes,
                     (D,), (F32,), tm)[0]


def _deinterleave(n_heads):
    within = np.concatenate([np.arange(0, ATT_HEAD_DIM, 2), np.arange(1, ATT_HEAD_DIM, 2)])
    return (np.arange(n_heads)[:, None] * ATT_HEAD_DIM + within[None, :]).reshape(-1)


def _attn_head_order():
    return np.array([kv * ATT_GROUP + j for j in range(ATT_GROUP) for kv in range(ATT_KV_HEADS)])


def _rope_tables(T):
    rows_n = T // GRID_W
    row = jnp.repeat(jnp.arange(rows_n), GRID_W).astype(F32)
    col = jnp.tile(jnp.arange(GRID_W), rows_n).astype(F32)
    axis_dim = ATT_HEAD_DIM // 2
    inv = ROPE_THETA ** (-jnp.arange(0, axis_dim, 2, dtype=F32) / axis_dim)
    ang = jnp.concatenate([row[:, None] * inv, col[:, None] * inv], axis=-1)
    cos, sin = jnp.cos(ang), jnp.sin(ang)
    reps = LANES // ATT_HEAD_DIM
    cs = jnp.tile(jnp.concatenate([cos, cos], axis=1), (1, reps))
    sn = jnp.tile(jnp.concatenate([-sin, sin], axis=1), (1, reps))
    return cs, sn


def _block_mean_matrix(width, group):
    idx = np.arange(width) // group
    return jnp.asarray((idx[:, None] == idx[None, :]).astype(np.float32) / group, BF16)


def kernel(x, c, ctx, c_ctx, mod_w, mod_b, norm_pre_mix, norm_post_mix, norm_pre_ffn, norm_post_ffn, even_w_in,
           gla_w_gate, gla_b_gate, gla_out_norm, att_q_norm, att_k_norm, even_w_out, odd_w_in, hgrn_lower_bounds,
           hgrn_out_norm, odd_w_out, ffn_w_gate, ffn_w_up, ffn_w_down):
    B, T, D = x.shape
    n_ctx = ctx.shape[1]
    depth = mod_w.shape[0]
    tm_lat = LAT_ROW_TILE
    tm_ctx = min(CTX_ROW_TILE, n_ctx)
    tm_post = min(POST_ROW_TILE, T)
    assert T % tm_lat == 0 and T % tm_post == 0 and n_ctx % tm_ctx == 0
    assert T % GRID_W == 0 and D % LANES == 0

    n_rows = -(-(B + 1) // 8) * 8
    cond = jnp.concatenate([c, c_ctx[None, :], jnp.zeros((n_rows - B - 1, D), F32)], axis=0)
    mods = _modulation(cond, mod_w, mod_b).reshape(depth * n_rows, 1, 6 * D)

    vec = lambda a: a.reshape(1, -1).astype(F32)
    bf = lambda a: a.astype(BF16)
    rope_tables = _rope_tables(T)
    ffn_gate16, ffn_up16, ffn_down16 = bf(ffn_w_gate), bf(ffn_w_up), bf(ffn_w_down)
    x_lat, x_ctx = x, ctx

    for l in range(depth):
        last = l == depth - 1
        j = l // 2
        lat_row = lambda b, base=l * n_rows: base + b
        ctx_row = lambda b, base=l * n_rows: base + B
        if l % 2 == 0:
            sizes = np.cumsum([0, GLA_HEADS * GLA_DK, GLA_HEADS * GLA_DK, GLA_HEADS * GLA_DV, GLA_HEADS * GLA_DV,
                               2 * GLA_GATE_RANK, ATT_Q_HEADS * ATT_HEAD_DIM, ATT_KV_HEADS * ATT_HEAD_DIM,
                               ATT_KV_HEADS * ATT_HEAD_DIM])
            o_gq, o_gk, o_gv, o_gog, o_gz, o_aq, o_ak, o_av = sizes[:-1]
            q_cols = o_aq + (_attn_head_order()[:, None] * ATT_HEAD_DIM
                             + _deinterleave(1)[None, :]).reshape(-1)
            k_cols = o_ak + _deinterleave(ATT_KV_HEADS)
            cols = np.concatenate([np.arange(o_gq, o_gz), q_cols, k_cols, np.arange(o_av, sizes[-1]),
                                   np.arange(o_gz, o_aq)])
            w_in = bf(even_w_in[j][:, cols])
            wgate = gla_w_gate[j]
            nk = GLA_HEADS * GLA_DK
            zeros = jnp.zeros((GLA_GATE_RANK, nk), F32)
            wg2 = bf(jnp.concatenate([jnp.concatenate([wgate[0], zeros], axis=1),
                                      jnp.concatenate([zeros, wgate[1]], axis=1)], axis=0))
            bg2 = gla_b_gate[j].reshape(1, 2 * nk).astype(F32)
            qn = vec(jnp.tile(att_q_norm[j][_deinterleave(1)], ATT_Q_HEADS))
            kn = vec(jnp.tile(att_k_norm[j][_deinterleave(1)], ATT_KV_HEADS))
            bdq = _block_mean_matrix(ATT_Q_HEADS * ATT_HEAD_DIM, ATT_HEAD_DIM)
            bdk = _block_mean_matrix(ATT_KV_HEADS * ATT_HEAD_DIM, ATT_HEAD_DIM)
            consts = (vec(norm_pre_mix[l]), w_in, wg2, bg2, qn, kn, bdq, bdk)
            lat = _pre_even(x_lat, mods, lat_row, consts, rope_tables, tm_lat)
            cx = _pre_even(x_ctx, mods, ctx_row, consts, None, tm_ctx)
            pick = lambda o: dict(zip(_SEG_FIELDS, (o[0], o[1], o[1], o[4], o[5], o[2], o[3])))
            mix_a = _scan(pick(lat), pick(cx), vec(gla_out_norm[j]), hpg=LANES // GLA_DK, ctx_out=not last,
                          name="scan_gla")
            mix_b_lat = _attention(lat[6], [(cx[7], cx[8]), (lat[7], lat[8])], "attention_lat")
            mixes_lat = (mix_a[0], mix_b_lat)
            if not last:
                mixes_ctx = (mix_a[1], _attention(cx[6], [(cx[7], cx[8])], "attention_ctx"))
            n_a = GLA_HEADS * GLA_DV
            b_rows = n_a + (_attn_head_order()[:, None] * ATT_HEAD_DIM
                            + np.arange(ATT_HEAD_DIM)[None, :]).reshape(-1)
            wouts = (bf(even_w_out[j][:n_a]), bf(even_w_out[j][b_rows]))
        else:
            w_in = bf(odd_w_in[j])
            lbs = hgrn_lower_bounds.astype(F32)
            lat = _pre_odd(x_lat, mods, lat_row, vec(norm_pre_mix[l]), w_in, lbs, l, tm_lat, "pre_odd_lat")
            cx = _pre_odd(x_ctx, mods, ctx_row, vec(norm_pre_mix[l]), w_in, lbs, l, tm_ctx, "pre_odd_ctx",
                          with_q=not last)
            mix = _scan(lat, cx, vec(hgrn_out_norm[j]), hpg=LANES // HGRN_DF, ctx_out=not last, name="scan_hgrn")
            mixes_lat = (mix[0],)
            if not last:
                mixes_ctx = (mix[1],)
            wouts = (bf(odd_w_out[j]),)
        consts = (*wouts, vec(norm_post_mix[l]), vec(norm_pre_ffn[l]), vec(norm_post_ffn[l]),
                  (ffn_gate16, l), (ffn_up16, l), (ffn_down16, l))
        if not last:
            x_ctx = _post(x_ctx, mods, ctx_row, mixes_ctx, consts, tm_ctx, "post_ctx")
        x_lat = _post(x_lat, mods, lat_row, mixes_lat, consts, tm_post, "post_lat")
    return x_lat
```
